```python
import jax, jax.numpy as jnp
from jax import lax
import numpy as np

D_MODEL = 2048
BATCH = 4
SEQ = 4096
DEPTH = 2

HEAD_DIM = 128
N_HEADS_SB = 8
N_HEADS_FOX = 8
WIDTH_SB = N_HEADS_SB * HEAD_DIM
WIDTH_FOX = N_HEADS_FOX * HEAD_DIM
Q_BLOCK = 128
PEER_HEADS = 8
PEER_KEYS = 128
PEER_TOPK = 16
PEER_HALF = 128
PEER_QDIM = 2 * PEER_HALF
N_EXPERTS = PEER_KEYS * PEER_KEYS
TOKEN_CHUNK = 128
PLE_DIM = 256
EPS = 1e-6

IN_SIZES = [WIDTH_SB, WIDTH_SB, WIDTH_SB,
            WIDTH_FOX, WIDTH_FOX, WIDTH_FOX,
            N_HEADS_FOX,
            D_MODEL, D_MODEL]
IN_COLS = sum(IN_SIZES)
IN_SPLITS = [int(c) for c in np.cumsum(IN_SIZES)[:-1]]

kernel_name = "hybrid_sb_fox_peer_ple"


def _rmsnorm(t, g):
    t32 = t.astype(jnp.float32)
    y = t32 * lax.rsqrt(jnp.mean(t32 * t32, axis=-1, keepdims=True) + EPS)
    return (y * g.astype(jnp.float32)).astype(t.dtype)


def _heads(t, n_heads):
    b, s, _ = t.shape
    return t.reshape(b, s, n_heads, HEAD_DIM).transpose(0, 2, 1, 3)


def _blocks(t):
    b, h, s = t.shape[:3]
    t = t.reshape(b, h, s // Q_BLOCK, Q_BLOCK, *t.shape[3:])
    return jnp.moveaxis(t, 2, 0)


def _unblocks(o):
    nb, b, h, q, hd = o.shape
    return o.transpose(1, 0, 3, 2, 4).reshape(b, nb * q, h * hd)


def stick_breaking_attention(q, k, v):
    s_len = q.shape[2]
    scale = HEAD_DIM ** -0.5
    k32 = k.astype(jnp.float32)
    v32 = v.astype(jnp.float32)
    spos = jnp.arange(s_len)

    def block(args):
        qb, start = args
        z = jnp.einsum('bhqd,bhkd->bhqk', qb.astype(jnp.float32), k32) * scale
        tpos = start + jnp.arange(Q_BLOCK)
        strict = spos[None, :] < tpos[:, None]
        log_1m = jnp.where(strict, jax.nn.log_sigmoid(-z), 0.0)
        after = lax.cumsum(log_1m, axis=3, reverse=True) - log_1m
        a = jnp.where(strict, jnp.exp(jax.nn.log_sigmoid(z) + after), 0.0)
        return jnp.einsum('bhqk,bhkd->bhqd', a, v32)

    starts = jnp.arange(s_len // Q_BLOCK, dtype=jnp.int32) * Q_BLOCK
    o = lax.map(block, (_blocks(q), starts))
    return _unblocks(o).astype(v.dtype)


def forgetting_attention(q, k, v, log_f):
    s_len = q.shape[2]
    scale = HEAD_DIM ** -0.5
    k32 = k.astype(jnp.float32)
    v32 = v.astype(jnp.float32)
    f_cum = jnp.cumsum(log_f, axis=2)
    spos = jnp.arange(s_len)

    def block(args):
        qb, fq, start = args
        z = (jnp.einsum('bhqd,bhkd->bhqk', qb.astype(jnp.float32), k32) * scale
             + fq[..., :, None] - f_cum[:, :, None, :])
        tpos = start + jnp.arange(Q_BLOCK)
        causal = spos[None, :] <= tpos[:, None]
        w = jax.nn.softmax(jnp.where(causal, z, -jnp.inf), axis=-1)
        return jnp.einsum('bhqk,bhkd->bhqd', w, v32)

    starts = jnp.arange(s_len // Q_BLOCK, dtype=jnp.int32) * Q_BLOCK
    o = lax.map(block, (_blocks(q), _blocks(f_cum), starts))
    return _unblocks(o).astype(v.dtype)


def peer_ffn(xn, w_query, sub_keys, expert_u, expert_v):
    b, s, d = xn.shape
    q = (xn @ w_query).astype(jnp.float32).reshape(b, s, PEER_HEADS, 2, PEER_HALF)
    scores = jnp.einsum('bshpc,hpkc->bshpk', q, sub_keys.astype(jnp.float32))
    s1, i1 = lax.top_k(scores[..., 0, :], PEER_TOPK)
    s2, i2 = lax.top_k(scores[..., 1, :], PEER_TOPK)
    cand_s = (s1[..., :, None] + s2[..., None, :]).reshape(b, s, PEER_HEADS, PEER_TOPK * PEER_TOPK)
    cand_i = (i1[..., :, None] * PEER_KEYS + i2[..., None, :]).reshape(b, s, PEER_HEADS, PEER_TOPK * PEER_TOPK)
    top_s, pos = lax.top_k(cand_s, PEER_TOPK)
    idx = jnp.take_along_axis(cand_i, pos, axis=-1)
    gate = jax.nn.softmax(top_s, axis=-1).astype(xn.dtype)

    n_chunks = (b * s) // TOKEN_CHUNK
    xc = xn.reshape(n_chunks, TOKEN_CHUNK, d)
    ic = idx.reshape(n_chunks, TOKEN_CHUNK, PEER_HEADS, PEER_TOPK)
    gc = gate.reshape(n_chunks, TOKEN_CHUNK, PEER_HEADS, PEER_TOPK)

    def chunk(args):
        xb, ib, gb = args
        hidden = jax.nn.gelu(jnp.einsum('chkd,cd->chk', expert_u[ib], xb), approximate=False)
        return jnp.einsum('chk,chkd->cd', gb * hidden, expert_v[ib])

    out = lax.map(chunk, (xc, ic, gc))
    return out.reshape(b, s, d)


def setup_inputs(seed: int = 0) -> dict:
    key = jax.random.key(seed)
    ks = jax.random.split(key, 20)

    def nrm(k, shape, scale):
        return jax.random.normal(k, shape, jnp.float32) * scale

    return {
        "x": nrm(ks[0], (BATCH, SEQ, D_MODEL), 1.0),
        "p": nrm(ks[1], (DEPTH, BATCH, SEQ, PLE_DIM), 1.0),
        "norm_mix_g": 1.0 + nrm(ks[2], (DEPTH, D_MODEL), 0.02),
        "w_in": nrm(ks[3], (DEPTH, D_MODEL, IN_COLS), D_MODEL ** -0.5),
        "b_forget": jnp.linspace(1.0, 5.0, N_HEADS_FOX, dtype=jnp.float32)[None, :]
                    + nrm(ks[4], (DEPTH, N_HEADS_FOX), 0.1),
        "w_branch_sb": nrm(ks[5], (DEPTH, WIDTH_SB, D_MODEL), WIDTH_SB ** -0.5),
        "w_branch_fox": nrm(ks[6], (DEPTH, WIDTH_FOX, D_MODEL), WIDTH_FOX ** -0.5),
        "w_out": nrm(ks[7], (DEPTH, D_MODEL, D_MODEL), D_MODEL ** -0.5),
        "norm_ffn_g": 1.0 + nrm(ks[8], (DEPTH, D_MODEL), 0.02),
        "w_query": nrm(ks[9], (DEPTH, D_MODEL, PEER_HEADS * PEER_QDIM), D_MODEL ** -0.5),
        "sub_keys": nrm(ks[10], (DEPTH, PEER_HEADS, 2, PEER_KEYS, PEER_HALF), PEER_HALF ** -0.5),
        "expert_u": nrm(ks[11], (DEPTH, N_EXPERTS, D_MODEL), D_MODEL ** -0.5),
        "expert_v": nrm(ks[12], (DEPTH, N_EXPERTS, D_MODEL), (PEER_HEADS * PEER_TOPK) ** -0.5),
        "norm_ple_g": 1.0 + nrm(ks[13], (DEPTH, D_MODEL), 0.02),
        "w_ple": nrm(ks[14], (DEPTH, PLE_DIM, D_MODEL), PLE_DIM ** -0.5),
        "w_ple_gate": nrm(ks[15], (DEPTH, D_MODEL, D_MODEL), D_MODEL ** -0.5),
        "final_norm_g": 1.0 + nrm(ks[16], (D_MODEL,), 0.02),
    }


def reference(x, p, norm_mix_g, w_in, b_forget, w_branch_sb, w_branch_fox, w_out,
              norm_ffn_g, w_query, sub_keys, expert_u, expert_v,
              norm_ple_g, w_ple, w_ple_gate, final_norm_g):
    h = x
    for i in range(DEPTH):
        xn = _rmsnorm(h, norm_mix_g[i])
        proj = xn @ w_in[i]
        q_sb, k_sb, v_sb, q_fx, k_fx, v_fx, f_logit, g_sb, g_fx = jnp.split(proj, IN_SPLITS, axis=-1)

        o_sb = stick_breaking_attention(_heads(q_sb, N_HEADS_SB), _heads(k_sb, N_HEADS_SB),
                                        _heads(v_sb, N_HEADS_SB))
        log_f = jax.nn.log_sigmoid(f_logit.astype(jnp.float32)
                                   + b_forget[i].astype(jnp.float32)).transpose(0, 2, 1)
        o_fx = forgetting_attention(_heads(q_fx, N_HEADS_FOX), _heads(k_fx, N_HEADS_FOX),
                                    _heads(v_fx, N_HEADS_FOX), log_f)

        merged = (jax.nn.sigmoid(g_sb) * (o_sb @ w_branch_sb[i])
                  + jax.nn.sigmoid(g_fx) * (o_fx @ w_branch_fox[i]))
        h = h + merged @ w_out[i]

        h = h + peer_ffn(_rmsnorm(h, norm_ffn_g[i]), w_query[i], sub_keys[i], expert_u[i], expert_v[i])

        ple_gate = jax.nn.sigmoid(_rmsnorm(h, norm_ple_g[i]) @ w_ple_gate[i])
        h = h + ple_gate * (p[i] @ w_ple[i])
    return _rmsnorm(h, final_norm_g)
```

```python
import functools

import jax
import jax.numpy as jnp
from jax import lax
from jax.experimental import pallas as pl
from jax.experimental.pallas import tpu as pltpu

EPS = 1e-6
HEAD_DIM = 128
N_HEADS = 8
WIDTH = N_HEADS * HEAD_DIM
PEER_HEADS = 8
PEER_KEYS = 128
PEER_TOPK = 16
SCALE = HEAD_DIM ** -0.5
LANE = 128
V7X_VMEM_LIMIT_BYTES = 56 * 1024 * 1024
NEG_BIG = -1e30
SB_DEAD_LOG = -112.0

F32 = jnp.float32
BF16 = jnp.bfloat16
_NT = (((1,), (1,)), ((), ()))


def _cparams(semantics):
    return pltpu.CompilerParams(dimension_semantics=semantics,
                                vmem_limit_bytes=V7X_VMEM_LIMIT_BYTES)


def _resident(block_shape, index_map):
    return pl.BlockSpec(block_shape, index_map, pipeline_mode=pl.Buffered(1))


def _rms(x, g):
    return x * lax.rsqrt(jnp.mean(x * x, axis=-1, keepdims=True) + EPS) * g


def _sigmoid(x):
    return 1.0 / (1.0 + jnp.exp(-x))


def _softplus(x):
    return jnp.maximum(x, 0.0) + jnp.log1p(jnp.exp(-jnp.abs(x)))


def _inproj_kernel(h_ref, g_ref, w_ref, wf_ref, o_ref, f_ref, xn_ref):
    j = pl.program_id(1)

    @pl.when(j == 0)
    def _():
        xn = _rms(h_ref[...], g_ref[...]).astype(BF16)
        xn_ref[...] = xn
        f_ref[...] = jnp.dot(xn, wf_ref[...], preferred_element_type=F32)

    acc = jnp.dot(xn_ref[...], w_ref[...], preferred_element_type=F32)
    is_q = jnp.logical_or(j == 0, j == 3)
    is_gate = j >= 6

    @pl.when(is_gate)
    def _():
        o_ref[...] = _sigmoid(acc).astype(o_ref.dtype)

    @pl.when(jnp.logical_not(is_gate))
    def _():
        o_ref[...] = (acc * jnp.where(is_q, SCALE, 1.0)).astype(o_ref.dtype)


def _inproj(h, g, w_main, w_f, *, tm):
    t, d = h.shape
    n = w_main.shape[1]
    tn = WIDTH
    return pl.pallas_call(
        _inproj_kernel,
        grid=(t // tm, n // tn),
        in_specs=[
            pl.BlockSpec((tm, d), lambda i, j: (i, 0)),
            pl.BlockSpec((1, d), lambda i, j: (0, 0)),
            pl.BlockSpec((d, tn), lambda i, j: (0, j)),
            pl.BlockSpec((d, LANE), lambda i, j: (0, 0)),
        ],
        out_specs=[
            pl.BlockSpec((tm, tn), lambda i, j: (i, j)),
            pl.BlockSpec((tm, LANE), lambda i, j: (i, 0)),
        ],
        out_shape=[
            jax.ShapeDtypeStruct((t, n), BF16),
            jax.ShapeDtypeStruct((t, LANE), F32),
        ],
        scratch_shapes=[pltpu.VMEM((tm, d), BF16)],
        compiler_params=_cparams(("arbitrary", "arbitrary")),
        name="inproj",
    )(h, g, w_main, w_f)


def _split3(x):
    hi = x.astype(BF16)
    r = x - hi.astype(F32)
    mid = r.astype(BF16)
    lo = (r - mid.astype(F32)).astype(BF16)
    return hi, mid, lo


def _fcum_kernel(x_ref, b_ref, o_ref, *, chunk):
    s_len = x_ref.shape[-1]
    x = x_ref[...] + b_ref[...]
    lf = -_softplus(-x)
    row = lax.broadcasted_iota(jnp.int32, (chunk, chunk), 0)
    col = lax.broadcasted_iota(jnp.int32, (chunk, chunk), 1)
    tri = jnp.where(row <= col, 1.0, 0.0).astype(BF16)
    carry = jnp.zeros((x.shape[0], 1), F32)
    for c in range(s_len // chunk):
        hi, mid, lo = _split3(lf[:, c * chunk:(c + 1) * chunk])
        cs = (jnp.dot(hi, tri, preferred_element_type=F32)
              + jnp.dot(mid, tri, preferred_element_type=F32)
              + jnp.dot(lo, tri, preferred_element_type=F32)) + carry
        o_ref[:, c * chunk:(c + 1) * chunk] = cs
        carry = cs[:, chunk - 1:chunk]


def _fcum(flog_t, b_col):
    b, nh, s = flog_t.shape
    return pl.pallas_call(
        functools.partial(_fcum_kernel, chunk=min(512, s)),
        grid=(b,),
        in_specs=[
            pl.BlockSpec((None, nh, s), lambda i: (i, 0, 0)),
            pl.BlockSpec((nh, 1), lambda i: (0, 0)),
        ],
        out_specs=pl.BlockSpec((None, nh, s), lambda i: (i, 0, 0)),
        out_shape=jax.ShapeDtypeStruct((b, nh, s), F32),
        compiler_params=_cparams(("arbitrary",)),
        name="forget_cumsum",
    )(flog_t, b_col)


def _fox_kernel(q_ref, k_ref, v_ref, fq_ref, fk_ref, o_ref, m_ref, l_ref, acc_ref, *, tq):
    qi = pl.program_id(2)
    q = q_ref[...]
    row = lax.broadcasted_iota(jnp.int32, (tq, tq), 0)
    col = lax.broadcasted_iota(jnp.int32, (tq, tq), 1)
    fq = jnp.sum(jnp.where(row == col, fq_ref[...], 0.0), axis=1, keepdims=True)

    def scores(j):
        ks = pl.multiple_of(j * tq, tq)
        k = k_ref[pl.ds(ks, tq), :]
        s = lax.dot_general(q, k, _NT, preferred_element_type=F32)
        return s + fq - fk_ref[j], v_ref[pl.ds(ks, tq), :]

    s, v = scores(qi)
    s = jnp.where(col <= row, s, NEG_BIG)
    m = jnp.max(s, axis=1, keepdims=True)
    p = jnp.exp(s - m)
    m_ref[...] = m
    l_ref[...] = jnp.sum(p, axis=1, keepdims=True)
    acc_ref[...] = jnp.dot(p.astype(BF16), v, preferred_element_type=F32)

    def body(j, carry):
        s, v = scores(j)
        m_prev = m_ref[...]
        m_new = jnp.maximum(m_prev, jnp.max(s, axis=1, keepdims=True))
        alpha = jnp.exp(m_prev - m_new)
        p = jnp.exp(s - m_new)
        l_ref[...] = alpha * l_ref[...] + jnp.sum(p, axis=1, keepdims=True)
        acc_ref[...] = alpha * acc_ref[...] + jnp.dot(p.astype(BF16), v, preferred_element_type=F32)
        m_ref[...] = m_new
        return carry

    lax.fori_loop(0, qi, body, 0)
    o_ref[...] = (acc_ref[...] / l_ref[...]).astype(o_ref.dtype)


def _fox(proj3, fcum4, *, tq):
    b, s, _ = proj3.shape
    nq = s // tq
    qb, kb, vb = 3 * N_HEADS, 4 * N_HEADS, 5 * N_HEADS
    fk5 = fcum4.reshape(b, N_HEADS, nq, 1, tq)
    return pl.pallas_call(
        functools.partial(_fox_kernel, tq=tq),
        grid=(b, N_HEADS, nq),
        in_specs=[
            pl.BlockSpec((None, tq, HEAD_DIM), lambda bi, h, qi: (bi, qi, qb + h)),
            pl.BlockSpec((None, s, HEAD_DIM), lambda bi, h, qi: (bi, 0, kb + h)),
            pl.BlockSpec((None, s, HEAD_DIM), lambda bi, h, qi: (bi, 0, vb + h)),
            pl.BlockSpec((None, None, 1, tq), lambda bi, h, qi: (bi, h, 0, qi)),
            pl.BlockSpec((None, None, nq, 1, tq), lambda bi, h, qi: (bi, h, 0, 0, 0)),
        ],
        out_specs=pl.BlockSpec((None, tq, HEAD_DIM), lambda bi, h, qi: (bi, qi, h)),
        out_shape=jax.ShapeDtypeStruct((b, s, WIDTH), BF16),
        scratch_shapes=[
            pltpu.VMEM((tq, 1), F32),
            pltpu.VMEM((tq, 1), F32),
            pltpu.VMEM((tq, HEAD_DIM), F32),
        ],
        compiler_params=_cparams(("arbitrary", "arbitrary", "arbitrary")),
        name="fox_attention",
    )(proj3, proj3, proj3, fcum4, fk5)


def _sb_kernel(q_ref, k_ref, v_ref, o_ref, acc_ref, c_ref, *, tq):
    qi = pl.program_id(2)
    q = q_ref[...]
    nsub = tq // LANE
    r128 = lax.broadcasted_iota(jnp.int32, (LANE, LANE), 0)
    c128 = lax.broadcasted_iota(jnp.int32, (LANE, LANE), 1)
    m_strict = jnp.where(r128 > c128, 1.0, 0.0).astype(BF16)
    m2 = jnp.concatenate([m_strict, m_strict], axis=0)
    row = lax.broadcasted_iota(jnp.int32, (tq, tq), 0)
    col = lax.broadcasted_iota(jnp.int32, (tq, tq), 1)

    def window(w, masked):
        ks = pl.multiple_of(w * tq, tq)
        k = k_ref[pl.ds(ks, tq), :]
        v = v_ref[pl.ds(ks, tq), :]
        z = lax.dot_general(q, k, _NT, preferred_element_type=F32)
        sp = _softplus(z)
        logb = z - sp
        lm = -sp
        if masked:
            valid = col < row
            lm = jnp.where(valid, lm, 0.0)
        run = c_ref[...]
        parts = [None] * nsub
        for sb in reversed(range(nsub)):
            sl = slice(sb * LANE, (sb + 1) * LANE)
            lb = lm[:, sl]
            hi = lb.astype(BF16)
            lo = (lb - hi.astype(F32)).astype(BF16)
            after = jnp.dot(jnp.concatenate([hi, lo], axis=1), m2, preferred_element_type=F32)
            a = jnp.exp(logb[:, sl] + after + run)
            if masked:
                a = jnp.where(valid[:, sl], a, 0.0)
            parts[sb] = a.astype(BF16)
            run = run + after[:, 0:1] + lb[:, 0:1]
        acc_ref[...] += jnp.dot(jnp.concatenate(parts, axis=1), v, preferred_element_type=F32)
        c_ref[...] = run
        return jnp.max(run)

    acc_ref[...] = jnp.zeros_like(acc_ref)
    c_ref[...] = jnp.zeros_like(c_ref)
    cmax0 = window(qi, True)

    def cond(carry):
        w, cmax = carry
        return jnp.logical_and(w >= 0, cmax > SB_DEAD_LOG)

    def body(carry):
        w, _ = carry
        return w - 1, window(w, False)

    lax.while_loop(cond, body, (qi - 1, cmax0))
    o_ref[...] = acc_ref[...].astype(o_ref.dtype)


def _sb(proj3, *, tq):
    b, s, _ = proj3.shape
    nq = s // tq
    qb, kb, vb = 0, N_HEADS, 2 * N_HEADS
    return pl.pallas_call(
        functools.partial(_sb_kernel, tq=tq),
        grid=(b, N_HEADS, nq),
        in_specs=[
            pl.BlockSpec((None, tq, HEAD_DIM), lambda bi, h, qi: (bi, qi, qb + h)),
            pl.BlockSpec((None, s, HEAD_DIM), lambda bi, h, qi: (bi, 0, kb + h)),
            pl.BlockSpec((None, s, HEAD_DIM), lambda bi, h, qi: (bi, 0, vb + h)),
        ],
        out_specs=pl.BlockSpec((None, tq, HEAD_DIM), lambda bi, h, qi: (bi, qi, h)),
        out_shape=jax.ShapeDtypeStruct((b, s, WIDTH), BF16),
        scratch_shapes=[
            pltpu.VMEM((tq, HEAD_DIM), F32),
            pltpu.VMEM((tq, 1), F32),
        ],
        compiler_params=_cparams(("arbitrary", "arbitrary", "arbitrary")),
        name="sb_attention",
    )(proj3, proj3, proj3)


def _merge_kernel(osb_ref, ofx_ref, gsb_ref, gfx_ref, h_ref, wsb_ref, wfx_ref, wout_ref, gn_ref,
                  h1_ref, xn_ref):
    ysb = jnp.dot(osb_ref[...], wsb_ref[...], preferred_element_type=F32)
    yfx = jnp.dot(ofx_ref[...], wfx_ref[...], preferred_element_type=F32)
    merged = gsb_ref[...].astype(F32) * ysb + gfx_ref[...].astype(F32) * yfx
    h1 = h_ref[...] + jnp.dot(merged.astype(BF16), wout_ref[...], preferred_element_type=F32)
    h1_ref[...] = h1
    xn_ref[...] = _rms(h1, gn_ref[...]).astype(BF16)


def _merge(o_sb, o_fx, proj, h, w_sb, w_fx, w_out, g_ffn, *, tm):
    t, d = h.shape
    gate_blk = 6 * WIDTH // d
    return pl.pallas_call(
        _merge_kernel,
        grid=(t // tm,),
        in_specs=[
            pl.BlockSpec((tm, WIDTH), lambda i: (i, 0)),
            pl.BlockSpec((tm, WIDTH), lambda i: (i, 0)),
            pl.BlockSpec((tm, d), lambda i: (i, gate_blk)),
            pl.BlockSpec((tm, d), lambda i: (i, gate_blk + 1)),
            pl.BlockSpec((tm, d), lambda i: (i, 0)),
            _resident((WIDTH, d), lambda i: (0, 0)),
            _resident((WIDTH, d), lambda i: (0, 0)),
            _resident((d, d), lambda i: (0, 0)),
            pl.BlockSpec((1, d), lambda i: (0, 0)),
        ],
        out_specs=[
            pl.BlockSpec((tm, d), lambda i: (i, 0)),
            pl.BlockSpec((tm, d), lambda i: (i, 0)),
        ],
        out_shape=[
            jax.ShapeDtypeStruct((t, d), F32),
            jax.ShapeDtypeStruct((t, d), BF16),
        ],
        compiler_params=_cparams(("arbitrary",)),
        name="merge_outproj",
    )(o_sb, o_fx, proj, proj, h, w_sb, w_fx, w_out, g_ffn)


def _qproj_kernel(x_ref, w_ref, o_ref):
    acc = jnp.dot(x_ref[...], w_ref[...], preferred_element_type=F32)
    for s in range(o_ref.shape[0]):
        o_ref[s] = acc[:, s * LANE:(s + 1) * LANE].astype(o_ref.dtype)


def _qproj(xn, w_q, *, tm):
    t, d = xn.shape
    n = w_q.shape[1]
    return pl.pallas_call(
        _qproj_kernel,
        grid=(t // tm,),
        in_specs=[
            pl.BlockSpec((tm, d), lambda i: (i, 0)),
            _resident((d, n), lambda i: (0, 0)),
        ],
        out_specs=pl.BlockSpec((n // LANE, tm, LANE), lambda i: (0, i, 0)),
        out_shape=jax.ShapeDtypeStruct((n // LANE, t, LANE), BF16),
        compiler_params=_cparams(("arbitrary",)),
        name="peer_query",
    )(xn, w_q)


def _erf(x):
    return lax.erf(x)


def _top16_rows(x_ref, rk_ref, tv_ref, iota_f):
    rk_ref[...] = jnp.full(rk_ref.shape, float(PEER_TOPK), F32)

    def rnd(a, carry):
        x = x_ref[...]
        m = jnp.max(x, axis=0, keepdims=True)
        idx = jnp.min(jnp.where(x == m, iota_f, float(PEER_KEYS)), axis=0, keepdims=True)
        sel = iota_f == idx
        rk_ref[...] = jnp.where(sel, a.astype(F32), rk_ref[...])
        x_ref[...] = jnp.where(sel, -jnp.inf, x)
        tv_ref[pl.ds(a, 1), :] = m
        return carry

    lax.fori_loop(0, PEER_TOPK, rnd, 0)


def _pair_candidates(tv_ref):
    s2lo = tv_ref[1, 0:8, :]
    groups = [tv_ref[0, a:a + 1, :] + s2lo for a in range(8)]
    groups.append(tv_ref[0, 0:1, :] + tv_ref[1, 8:16, :])
    groups.append(tv_ref[0, 8:16, :] + tv_ref[1, 0:1, :])
    return jnp.concatenate(groups, axis=0)


def _router(q_ref, sk_ref, n1_ref, r2_ref, e1_ref, e2_ref, s_ref, x_ref, rk_ref, tv_ref,
            cand_ref, sel_ref, *, tt):
    iota_f = lax.broadcasted_iota(jnp.int32, (PEER_KEYS, tt), 0).astype(F32)
    r80 = lax.broadcasted_iota(jnp.int32, (80, tt), 0)
    pos = jnp.where(r80 < 64, (r80 >> 3) * 16 + (r80 & 7),
                    jnp.where(r80 < 72, r80 - 56, (r80 - 64) * 16)).astype(F32)

    def head(h, carry):
        for half in range(2):
            st = lax.dot_general(sk_ref[2 * h + half], q_ref[2 * h + half], _NT,
                                 preferred_element_type=F32)
            s_ref[half] = st
            x_ref[...] = st
            _top16_rows(x_ref, rk_ref.at[half], tv_ref.at[half], iota_f)

        cand_ref[...] = _pair_candidates(tv_ref)
        sel_ref[...] = jnp.zeros(sel_ref.shape, F32)

        def rnd(_, c):
            cnd = cand_ref[...]
            m = jnp.max(cnd, axis=0, keepdims=True)
            pmin = jnp.min(jnp.where(cnd == m, pos, 1e9), axis=0, keepdims=True)
            sel = pos == pmin
            sel_ref[...] = jnp.where(sel, 1.0, sel_ref[...])
            cand_ref[...] = jnp.where(sel, -jnp.inf, cnd)
            return c

        lax.fori_loop(0, PEER_TOPK, rnd, 0)

        selm = sel_ref[...]
        m1 = tv_ref[0, 0:1, :]
        m2 = tv_ref[1, 0:1, :]
        zsum = jnp.sum(jnp.where(selm > 0.0, jnp.exp(_pair_candidates(tv_ref) - (m1 + m2)), 0.0),
                       axis=0, keepdims=True)
        rk1 = rk_ref[0]
        n1 = jnp.zeros((PEER_KEYS, tt), F32)
        for a in range(PEER_TOPK):
            if a < 8:
                na = jnp.sum(selm[a * 8:(a + 1) * 8], axis=0, keepdims=True)
                if a == 0:
                    na = na + jnp.sum(selm[64:72], axis=0, keepdims=True)
            else:
                na = selm[64 + a:65 + a]
            n1 = jnp.where(rk1 == float(a), na, n1)
        n1_ref[h] = n1
        r2_ref[h] = rk_ref[1]
        e1_ref[h] = jnp.exp(s_ref[0] - m1) * (1.0 / zsum)
        e2_ref[h] = jnp.exp(s_ref[1] - m2)
        return carry

    lax.fori_loop(0, PEER_HEADS, head, 0)


def _peer_kernel(xn_ref, q_ref, sk_ref, u_ref, vt_ref, o_ref,
                 n1_ref, r2_ref, e1_ref, e2_ref, acc_ref,
                 s_ref, x_ref, rk_ref, tv_ref, cand_ref, sel_ref, *, tt, ec):
    c = pl.program_id(1)

    @pl.when(c == 0)
    def _():
        _router(q_ref, sk_ref, n1_ref, r2_ref, e1_ref, e2_ref, s_ref, x_ref, rk_ref, tv_ref,
                cand_ref, sel_ref, tt=tt)
        acc_ref[...] = jnp.zeros_like(acc_ref)

    hid = lax.dot_general(u_ref[...], xn_ref[...], _NT, preferred_element_type=F32)
    act = 0.5 * hid * (1.0 + _erf(hid * (2.0 ** -0.5)))
    groups = ec // PEER_KEYS
    rows = []
    for r in range(groups):
        i = c * groups + r
        w = jnp.zeros((PEER_KEYS, tt), F32)
        for h in range(PEER_HEADS):
            n1 = n1_ref[h, pl.ds(i, 1), :]
            e1 = e1_ref[h, pl.ds(i, 1), :]
            w = w + jnp.where(r2_ref[h] < n1, e1 * e2_ref[h], 0.0)
        rows.append((w * act[r * PEER_KEYS:(r + 1) * PEER_KEYS]).astype(BF16))
    a_t = jnp.concatenate(rows, axis=0)
    acc_ref[...] += jnp.dot(vt_ref[...], a_t, preferred_element_type=F32)

    @pl.when(c == pl.num_programs(1) - 1)
    def _():
        o_ref[...] = acc_ref[...].T


def _peer(xn, q_hm, sk, u, vt, *, tt, ec):
    t, d = xn.shape
    ne = u.shape[0]
    nslab = q_hm.shape[0]
    head_arr = pltpu.VMEM((PEER_HEADS, PEER_KEYS, tt), F32)
    return pl.pallas_call(
        functools.partial(_peer_kernel, tt=tt, ec=ec),
        grid=(t // tt, ne // ec),
        in_specs=[
            pl.BlockSpec((tt, d), lambda i, c: (i, 0)),
            pl.BlockSpec((nslab, tt, LANE), lambda i, c: (0, i, 0)),
            _resident((nslab, PEER_KEYS, LANE), lambda i, c: (0, 0, 0)),
            pl.BlockSpec((ec, d), lambda i, c: (c, 0)),
            pl.BlockSpec((d, ec), lambda i, c: (0, c)),
        ],
        out_specs=pl.BlockSpec((tt, d), lambda i, c: (i, 0)),
        out_shape=jax.ShapeDtypeStruct((t, d), F32),
        scratch_shapes=[
            head_arr, head_arr, head_arr, head_arr,
            pltpu.VMEM((d, tt), F32),
            pltpu.VMEM((2, PEER_KEYS, tt), F32),
            pltpu.VMEM((PEER_KEYS, tt), F32),
            pltpu.VMEM((2, PEER_KEYS, tt), F32),
            pltpu.VMEM((2, PEER_TOPK, tt), F32),
            pltpu.VMEM((80, tt), F32),
            pltpu.VMEM((80, tt), F32),
        ],
        compiler_params=_cparams(("arbitrary", "arbitrary")),
        name="peer",
    )(xn, q_hm, sk, u, vt)


def _ple_kernel(h1_ref, po_ref, p_ref, g_ref, wg_ref, wp_ref, gf_ref, o_ref, *, final):
    h2 = h1_ref[...] + po_ref[...]
    xn = _rms(h2, g_ref[...]).astype(BF16)
    gate = _sigmoid(jnp.dot(xn, wg_ref[...], preferred_element_type=F32))
    h3 = h2 + gate * jnp.dot(p_ref[...].astype(BF16), wp_ref[...], preferred_element_type=F32)
    if final:
        h3 = _rms(h3, gf_ref[...])
    o_ref[...] = h3


def _ple(h1, peer_out, p, g_ple, w_gate, w_ple, g_final, *, tm, final):
    t, d = h1.shape
    pd = p.shape[1]
    return pl.pallas_call(
        functools.partial(_ple_kernel, final=final),
        grid=(t // tm,),
        in_specs=[
            pl.BlockSpec((tm, d), lambda i: (i, 0)),
            pl.BlockSpec((tm, d), lambda i: (i, 0)),
            pl.BlockSpec((tm, pd), lambda i: (i, 0)),
            pl.BlockSpec((1, d), lambda i: (0, 0)),
            _resident((d, d), lambda i: (0, 0)),
            _resident((pd, d), lambda i: (0, 0)),
            pl.BlockSpec((1, d), lambda i: (0, 0)),
        ],
        out_specs=pl.BlockSpec((tm, d), lambda i: (i, 0)),
        out_shape=jax.ShapeDtypeStruct((t, d), F32),
        compiler_params=_cparams(("arbitrary",)),
        name="ple",
    )(h1, peer_out, p, g_ple, w_gate, w_ple, g_final)


def _pick(n, pref):
    return pref if n % pref == 0 else n


def _layer(h, b, s, p_i, norm_mix_g, w_in, b_forget, w_branch_sb, w_branch_fox, w_out,
           norm_ffn_g, w_query, sub_keys, expert_u, expert_v, norm_ple_g, w_ple, w_ple_gate,
           final_norm_g, final):
    t, d = h.shape
    n_qkv = 6 * WIDTH
    w_main = jnp.concatenate([w_in[:, :n_qkv], w_in[:, n_qkv + N_HEADS:]], axis=1).astype(BF16)
    w_f = jnp.pad(w_in[:, n_qkv:n_qkv + N_HEADS], ((0, 0), (0, LANE - N_HEADS))).astype(BF16)

    proj, flog = _inproj(h, norm_mix_g.reshape(1, d), w_main, w_f, tm=_pick(t, 512))
    proj3 = proj.reshape(b, s, proj.shape[1])

    flog_t = flog[:, :N_HEADS].reshape(b, s, N_HEADS).transpose(0, 2, 1)
    fcum = _fcum(flog_t, b_forget.reshape(N_HEADS, 1).astype(F32))
    tq = _pick(s, 512)
    o_fx = _fox(proj3, fcum.reshape(b, N_HEADS, 1, s), tq=tq)
    o_sb = _sb(proj3, tq=tq)

    h1, xn = _merge(o_sb.reshape(t, WIDTH), o_fx.reshape(t, WIDTH), proj, h,
                    w_branch_sb.astype(BF16), w_branch_fox.astype(BF16), w_out.astype(BF16),
                    norm_ffn_g.reshape(1, d), tm=_pick(t, 256))

    q_hm = _qproj(xn, w_query.astype(BF16), tm=_pick(t, 512))
    sk = sub_keys.reshape(2 * PEER_HEADS, PEER_KEYS, sub_keys.shape[-1]).astype(BF16)
    peer_out = _peer(xn, q_hm, sk, expert_u.astype(BF16), expert_v.T.astype(BF16),
                     tt=_pick(t, 512), ec=512)

    return _ple(h1, peer_out, p_i, norm_ple_g.reshape(1, d), w_ple_gate.astype(BF16),
                w_ple.astype(BF16), final_norm_g.reshape(1, d), tm=_pick(t, 256), final=final)


def kernel(x, p, norm_mix_g, w_in, b_forget, w_branch_sb, w_branch_fox, w_out, norm_ffn_g,
           w_query, sub_keys, expert_u, expert_v, norm_ple_g, w_ple, w_ple_gate, final_norm_g):
    b, s, d = x.shape
    depth = p.shape[0]
    h = x.reshape(b * s, d)
    for i in range(depth):
        h = _layer(h, b, s, p[i].reshape(b * s, -1), norm_mix_g[i], w_in[i], b_forget[i],
                   w_branch_sb[i], w_branch_fox[i], w_out[i], norm_ffn_g[i], w_query[i],
                   sub_keys[i], expert_u[i], expert_v[i], norm_ple_g[i], w_ple[i], w_ple_gate[i],
                   final_norm_g, final=(i == depth - 1))
    return h.reshape(b, s, d)
```

```python
import functools

import jax
import jax.numpy as jnp
from jax import lax
from jax.experimental import pallas as pl
from jax.experimental.pallas import tpu as pltpu

EPS = 1e-6
HEAD_DIM = 128
N_HEADS = 8
WIDTH = N_HEADS * HEAD_DIM
PEER_HEADS = 8
PEER_KEYS = 128
PEER_TOPK = 16
SCALE = HEAD_DIM ** -0.5
LANE = 128
BF16_ROWS = 16
V7X_VMEM_LIMIT_BYTES = 56 * 1024 * 1024
NEG_BIG = -1e30
SB_DEAD_LOG = -112.0

F32 = jnp.float32
BF16 = jnp.bfloat16
_NT = (((1,), (1,)), ((), ()))


def _cparams(semantics):
    return pltpu.CompilerParams(dimension_semantics=semantics,
                                vmem_limit_bytes=V7X_VMEM_LIMIT_BYTES)


def _resident(block_shape, index_map):
    return pl.BlockSpec(block_shape, index_map, pipeline_mode=pl.Buffered(1))


def _rms(x, g):
    return x * lax.rsqrt(jnp.mean(x * x, axis=-1, keepdims=True) + EPS) * g


def _sigmoid(x):
    return 1.0 / (1.0 + jnp.exp(-x))


def _softplus(x):
    return jnp.maximum(x, 0.0) + jnp.log1p(jnp.exp(-jnp.abs(x)))


def _inproj_kernel(h_ref, g_ref, w_ref, wf_ref, o_ref, f_ref, xn_ref):
    j = pl.program_id(1)

    @pl.when(j == 0)
    def _():
        xn = _rms(h_ref[...], g_ref[...]).astype(BF16)
        xn_ref[...] = xn
        f_ref[...] = jnp.dot(xn, wf_ref[...], preferred_element_type=F32)

    acc = jnp.dot(xn_ref[...], w_ref[...], preferred_element_type=F32)
    is_q = jnp.logical_or(j == 0, j == 3)
    is_gate = j >= 6

    @pl.when(is_gate)
    def _():
        o_ref[...] = _sigmoid(acc).astype(o_ref.dtype)

    @pl.when(jnp.logical_not(is_gate))
    def _():
        o_ref[...] = (acc * jnp.where(is_q, SCALE, 1.0)).astype(o_ref.dtype)


def _inproj(h, g, w_main, w_f, *, tm):
    t, d = h.shape
    n = w_main.shape[1]
    tn = WIDTH
    return pl.pallas_call(
        _inproj_kernel,
        grid=(t // tm, n // tn),
        in_specs=[
            pl.BlockSpec((tm, d), lambda i, j: (i, 0)),
            pl.BlockSpec((1, d), lambda i, j: (0, 0)),
            pl.BlockSpec((d, tn), lambda i, j: (0, j)),
            pl.BlockSpec((d, LANE), lambda i, j: (0, 0)),
        ],
        out_specs=[
            pl.BlockSpec((tm, tn), lambda i, j: (i, j)),
            pl.BlockSpec((tm, LANE), lambda i, j: (i, 0)),
        ],
        out_shape=[
            jax.ShapeDtypeStruct((t, n), BF16),
            jax.ShapeDtypeStruct((t, LANE), F32),
        ],
        scratch_shapes=[pltpu.VMEM((tm, d), BF16)],
        compiler_params=_cparams(("arbitrary", "arbitrary")),
        name="inproj",
    )(h, g, w_main, w_f)


def _split3(x):
    hi = x.astype(BF16)
    r = x - hi.astype(F32)
    mid = r.astype(BF16)
    lo = (r - mid.astype(F32)).astype(BF16)
    return hi, mid, lo


def _fcum_kernel(x_ref, b_ref, o_ref, *, chunk):
    s_len = x_ref.shape[-1]
    x = x_ref[...] + b_ref[...]
    lf = -_softplus(-x)
    row = lax.broadcasted_iota(jnp.int32, (chunk, chunk), 0)
    col = lax.broadcasted_iota(jnp.int32, (chunk, chunk), 1)
    tri = jnp.where(row <= col, 1.0, 0.0).astype(BF16)
    carry = jnp.zeros((x.shape[0], 1), F32)
    for c in range(s_len // chunk):
        hi, mid, lo = _split3(lf[:, c * chunk:(c + 1) * chunk])
        cs = (jnp.dot(hi, tri, preferred_element_type=F32)
              + jnp.dot(mid, tri, preferred_element_type=F32)
              + jnp.dot(lo, tri, preferred_element_type=F32)) + carry
        o_ref[:, c * chunk:(c + 1) * chunk] = cs
        carry = cs[:, chunk - 1:chunk]


def _fcum(flog_t, b_col):
    b, nh, s = flog_t.shape
    return pl.pallas_call(
        functools.partial(_fcum_kernel, chunk=min(512, s)),
        grid=(b,),
        in_specs=[
            pl.BlockSpec((None, nh, s), lambda i: (i, 0, 0)),
            pl.BlockSpec((nh, 1), lambda i: (0, 0)),
        ],
        out_specs=pl.BlockSpec((None, nh, s), lambda i: (i, 0, 0)),
        out_shape=jax.ShapeDtypeStruct((b, nh, s), F32),
        compiler_params=_cparams(("arbitrary",)),
        name="forget_cumsum",
    )(flog_t, b_col)


def _fox_kernel(q_ref, k_ref, v_ref, fq_ref, fk_ref, o_ref, m_ref, l_ref, acc_ref, *, tq, hpg):
    qi = pl.program_id(2)
    row = lax.broadcasted_iota(jnp.int32, (tq, tq), 0)
    col = lax.broadcasted_iota(jnp.int32, (tq, tq), 1)
    heads = range(hpg)
    cols = [slice(hh * HEAD_DIM, (hh + 1) * HEAD_DIM) for hh in heads]
    q = [q_ref[:, cols[hh]] for hh in heads]
    fq = [jnp.sum(jnp.where(row == col, fq_ref[hh], 0.0), axis=1, keepdims=True) for hh in heads]

    def scores(hh, j):
        ks = pl.multiple_of(j * tq, tq)
        k = k_ref[pl.ds(ks, tq), cols[hh]]
        s = lax.dot_general(q[hh], k, _NT, preferred_element_type=F32)
        return s + fq[hh] - fk_ref[hh, j], v_ref[pl.ds(ks, tq), cols[hh]]

    for hh in heads:
        s, v = scores(hh, qi)
        s = jnp.where(col <= row, s, NEG_BIG)
        m = jnp.max(s, axis=1, keepdims=True)
        p = jnp.exp(s - m)
        m_ref[hh] = m
        l_ref[hh] = jnp.sum(p, axis=1, keepdims=True)
        acc_ref[hh] = jnp.dot(p.astype(BF16), v, preferred_element_type=F32)

    def body(j, carry):
        for hh in heads:
            s, v = scores(hh, j)
            m_prev = m_ref[hh]
            m_new = jnp.maximum(m_prev, jnp.max(s, axis=1, keepdims=True))
            alpha = jnp.exp(m_prev - m_new)
            p = jnp.exp(s - m_new)
            l_ref[hh] = alpha * l_ref[hh] + jnp.sum(p, axis=1, keepdims=True)
            acc_ref[hh] = alpha * acc_ref[hh] + jnp.dot(p.astype(BF16), v,
                                                        preferred_element_type=F32)
            m_ref[hh] = m_new
        return carry

    lax.fori_loop(0, qi, body, 0)
    for hh in heads:
        o_ref[:, cols[hh]] = (acc_ref[hh] / l_ref[hh]).astype(o_ref.dtype)


def _fox(proj3, fcum4, *, tq, hpg=2):
    b, s, _ = proj3.shape
    nq = s // tq
    ng = N_HEADS // hpg
    wb = hpg * HEAD_DIM
    qb, kb, vb = 3 * ng, 4 * ng, 5 * ng
    fk5 = fcum4.reshape(b, N_HEADS, nq, 1, tq)
    return pl.pallas_call(
        functools.partial(_fox_kernel, tq=tq, hpg=hpg),
        grid=(b, ng, nq),
        in_specs=[
            pl.BlockSpec((None, tq, wb), lambda bi, g, qi: (bi, qi, qb + g)),
            pl.BlockSpec((None, s, wb), lambda bi, g, qi: (bi, 0, kb + g)),
            pl.BlockSpec((None, s, wb), lambda bi, g, qi: (bi, 0, vb + g)),
            pl.BlockSpec((None, hpg, 1, tq), lambda bi, g, qi: (bi, g, 0, qi)),
            pl.BlockSpec((None, hpg, nq, 1, tq), lambda bi, g, qi: (bi, g, 0, 0, 0)),
        ],
        out_specs=pl.BlockSpec((None, tq, wb), lambda bi, g, qi: (bi, qi, g)),
        out_shape=jax.ShapeDtypeStruct((b, s, WIDTH), BF16),
        scratch_shapes=[
            pltpu.VMEM((hpg, tq, 1), F32),
            pltpu.VMEM((hpg, tq, 1), F32),
            pltpu.VMEM((hpg, tq, HEAD_DIM), F32),
        ],
        compiler_params=_cparams(("arbitrary", "arbitrary", "arbitrary")),
        name="fox_attention",
    )(proj3, proj3, proj3, fcum4, fk5)


def _sb_kernel(q_ref, k_ref, v_ref, o_ref, acc_ref, c_ref, *, tq):
    qi = pl.program_id(2)
    q = q_ref[...]
    nsub = tq // LANE
    r128 = lax.broadcasted_iota(jnp.int32, (LANE, LANE), 0)
    c128 = lax.broadcasted_iota(jnp.int32, (LANE, LANE), 1)
    m_strict = jnp.where(r128 > c128, 1.0, 0.0).astype(BF16)
    m2 = jnp.concatenate([m_strict, m_strict], axis=0)
    row = lax.broadcasted_iota(jnp.int32, (tq, tq), 0)
    col = lax.broadcasted_iota(jnp.int32, (tq, tq), 1)

    def window(w, masked):
        ks = pl.multiple_of(w * tq, tq)
        k = k_ref[pl.ds(ks, tq), :]
        v = v_ref[pl.ds(ks, tq), :]
        z = lax.dot_general(q, k, _NT, preferred_element_type=F32)
        sp = _softplus(z)
        logb = z - sp
        lm = -sp
        if masked:
            valid = col < row
            lm = jnp.where(valid, lm, 0.0)
        run = c_ref[...]
        parts = [None] * nsub
        for sb in reversed(range(nsub)):
            sl = slice(sb * LANE, (sb + 1) * LANE)
            lb = lm[:, sl]
            hi = lb.astype(BF16)
            lo = (lb - hi.astype(F32)).astype(BF16)
            after = jnp.dot(jnp.concatenate([hi, lo], axis=1), m2, preferred_element_type=F32)
            a = jnp.exp(logb[:, sl] + after + run)
            if masked:
                a = jnp.where(valid[:, sl], a, 0.0)
            parts[sb] = a.astype(BF16)
            run = run + after[:, 0:1] + lb[:, 0:1]
        acc_ref[...] += jnp.dot(jnp.concatenate(parts, axis=1), v, preferred_element_type=F32)
        c_ref[...] = run
        return jnp.max(run)

    acc_ref[...] = jnp.zeros_like(acc_ref)
    c_ref[...] = jnp.zeros_like(c_ref)
    cmax0 = window(qi, True)

    def cond(carry):
        w, cmax = carry
        return jnp.logical_and(w >= 0, cmax > SB_DEAD_LOG)

    def body(carry):
        w, _ = carry
        return w - 1, window(w, False)

    lax.while_loop(cond, body, (qi - 1, cmax0))
    o_ref[...] = acc_ref[...].astype(o_ref.dtype)


def _sb(proj3, *, tq):
    b, s, _ = proj3.shape
    nq = s // tq
    qb, kb, vb = 0, N_HEADS, 2 * N_HEADS
    return pl.pallas_call(
        functools.partial(_sb_kernel, tq=tq),
        grid=(b, N_HEADS, nq),
        in_specs=[
            pl.BlockSpec((None, tq, HEAD_DIM), lambda bi, h, qi: (bi, qi, qb + h)),
            pl.BlockSpec((None, s, HEAD_DIM), lambda bi, h, qi: (bi, 0, kb + h)),
            pl.BlockSpec((None, s, HEAD_DIM), lambda bi, h, qi: (bi, 0, vb + h)),
        ],
        out_specs=pl.BlockSpec((None, tq, HEAD_DIM), lambda bi, h, qi: (bi, qi, h)),
        out_shape=jax.ShapeDtypeStruct((b, s, WIDTH), BF16),
        scratch_shapes=[
            pltpu.VMEM((tq, HEAD_DIM), F32),
            pltpu.VMEM((tq, 1), F32),
        ],
        compiler_params=_cparams(("arbitrary", "arbitrary", "arbitrary")),
        name="sb_attention",
    )(proj3, proj3, proj3)


def _merge_kernel(osb_ref, ofx_ref, gsb_ref, gfx_ref, h_ref, wsb_ref, wfx_ref, wout_ref, gn_ref,
                  h1_ref, xn_ref):
    ysb = jnp.dot(osb_ref[...], wsb_ref[...], preferred_element_type=F32)
    yfx = jnp.dot(ofx_ref[...], wfx_ref[...], preferred_element_type=F32)
    merged = gsb_ref[...].astype(F32) * ysb + gfx_ref[...].astype(F32) * yfx
    h1 = h_ref[...] + jnp.dot(merged.astype(BF16), wout_ref[...], preferred_element_type=F32)
    h1_ref[...] = h1
    xn_ref[...] = _rms(h1, gn_ref[...]).astype(BF16)


def _merge(o_sb, o_fx, proj, h, w_sb, w_fx, w_out, g_ffn, *, tm):
    t, d = h.shape
    gate_blk = 6 * WIDTH // d
    return pl.pallas_call(
        _merge_kernel,
        grid=(t // tm,),
        in_specs=[
            pl.BlockSpec((tm, WIDTH), lambda i: (i, 0)),
            pl.BlockSpec((tm, WIDTH), lambda i: (i, 0)),
            pl.BlockSpec((tm, d), lambda i: (i, gate_blk)),
            pl.BlockSpec((tm, d), lambda i: (i, gate_blk + 1)),
            pl.BlockSpec((tm, d), lambda i: (i, 0)),
            _resident((WIDTH, d), lambda i: (0, 0)),
            _resident((WIDTH, d), lambda i: (0, 0)),
            _resident((d, d), lambda i: (0, 0)),
            pl.BlockSpec((1, d), lambda i: (0, 0)),
        ],
        out_specs=[
            pl.BlockSpec((tm, d), lambda i: (i, 0)),
            pl.BlockSpec((tm, d), lambda i: (i, 0)),
        ],
        out_shape=[
            jax.ShapeDtypeStruct((t, d), F32),
            jax.ShapeDtypeStruct((t, d), BF16),
        ],
        compiler_params=_cparams(("arbitrary",)),
        name="merge_outproj",
    )(o_sb, o_fx, proj, proj, h, w_sb, w_fx, w_out, g_ffn)


def _qproj_kernel(x_ref, w_ref, o_ref):
    acc = jnp.dot(x_ref[...], w_ref[...], preferred_element_type=F32)
    for s in range(o_ref.shape[0]):
        o_ref[s] = acc[:, s * LANE:(s + 1) * LANE].astype(o_ref.dtype)


def _qproj(xn, w_q, *, tm):
    t, d = xn.shape
    n = w_q.shape[1]
    return pl.pallas_call(
        _qproj_kernel,
        grid=(t // tm,),
        in_specs=[
            pl.BlockSpec((tm, d), lambda i: (i, 0)),
            _resident((d, n), lambda i: (0, 0)),
        ],
        out_specs=pl.BlockSpec((n // LANE, tm, LANE), lambda i: (0, i, 0)),
        out_shape=jax.ShapeDtypeStruct((n // LANE, t, LANE), BF16),
        compiler_params=_cparams(("arbitrary",)),
        name="peer_query",
    )(xn, w_q)


def _erf(x):
    return lax.erf(x)


def _extract16(x_ref, mark_ref, order_f, on_round, *, break_ties):
    mark_ref[...] = jnp.full(mark_ref.shape, float(PEER_TOPK), F32)

    def rnd(a, carry):
        x = x_ref[...]
        m = jnp.max(x, axis=0, keepdims=True)
        sel = x == m
        if break_ties:
            first = jnp.min(jnp.where(sel, order_f, 1e9), axis=0, keepdims=True)
            sel = order_f == first
        mark_ref[...] = jnp.where(sel, a.astype(F32), mark_ref[...])
        x_ref[...] = jnp.where(sel, -jnp.inf, x)
        on_round(a, m)
        return carry

    lax.fori_loop(0, PEER_TOPK, rnd, 0)


def _top16_exact(load_x, x_ref, mark_ref, order_f, on_round):
    x_ref[...] = load_x()
    _extract16(x_ref, mark_ref, order_f, on_round, break_ties=False)
    taken = jnp.sum(jnp.where(mark_ref[...] < float(PEER_TOPK), 1.0, 0.0), axis=0, keepdims=True)
    clean = jnp.logical_and(jnp.max(taken) == float(PEER_TOPK), jnp.min(taken) == float(PEER_TOPK))

    @pl.when(jnp.logical_not(clean))
    def _():
        x_ref[...] = load_x()
        _extract16(x_ref, mark_ref, order_f, on_round, break_ties=True)


def _pair_candidates(tv_ref):
    s2lo = tv_ref[1, 0:8, :]
    groups = [tv_ref[0, a:a + 1, :] + s2lo for a in range(8)]
    groups.append(tv_ref[0, 0:1, :] + tv_ref[1, 8:16, :])
    groups.append(tv_ref[0, 8:16, :] + tv_ref[1, 0:1, :])
    return jnp.concatenate(groups, axis=0)


def _router(q_ref, sk_ref, n1_ref, r2_ref, e1_ref, e2_ref, s_ref, x_ref, rk_ref, tv_ref,
            cand_ref, sel_ref, *, tt):
    iota_f = lax.broadcasted_iota(jnp.int32, (PEER_KEYS, tt), 0).astype(F32)
    r80 = lax.broadcasted_iota(jnp.int32, (80, tt), 0)
    pos = jnp.where(r80 < 64, (r80 >> 3) * 16 + (r80 & 7),
                    jnp.where(r80 < 72, r80 - 56, (r80 - 64) * 16)).astype(F32)

    def head(h, carry):
        for half in range(2):
            st = lax.dot_general(sk_ref[2 * h + half], q_ref[2 * h + half], _NT,
                                 preferred_element_type=F32)
            s_ref[half] = st

            def keep_top_value(a, m, half=half):
                tv_ref[half, pl.ds(a, 1), :] = m

            _top16_exact(lambda half=half: s_ref[half], x_ref, rk_ref.at[half], iota_f,
                         keep_top_value)

        _top16_exact(lambda: _pair_candidates(tv_ref), cand_ref, sel_ref, pos, lambda a, m: None)
        selm = jnp.where(sel_ref[...] < float(PEER_TOPK), 1.0, 0.0)
        m1 = tv_ref[0, 0:1, :]
        m2 = tv_ref[1, 0:1, :]
        zsum = jnp.sum(jnp.where(selm > 0.0, jnp.exp(_pair_candidates(tv_ref) - (m1 + m2)), 0.0),
                       axis=0, keepdims=True)
        rk1 = rk_ref[0]
        n1 = jnp.zeros((PEER_KEYS, tt), F32)
        for a in range(PEER_TOPK):
            if a < 8:
                na = jnp.sum(selm[a * 8:(a + 1) * 8], axis=0, keepdims=True)
                if a == 0:
                    na = na + jnp.sum(selm[64:72], axis=0, keepdims=True)
            else:
                na = selm[64 + a:65 + a]
            n1 = jnp.where(rk1 == float(a), na, n1)
        n1_ref[h] = n1
        r2_ref[h] = rk_ref[1].astype(BF16)
        e1_ref[h] = jnp.exp(s_ref[0] - m1) * (1.0 / zsum)
        e2_ref[h] = jnp.exp(s_ref[1] - m2).astype(BF16)
        return carry

    lax.fori_loop(0, PEER_HEADS, head, 0)


def _peer_kernel(xn_ref, q_ref, sk_ref, u_ref, vt_ref, o_ref,
                 n1_ref, r2_ref, e1_ref, e2_ref, acc_ref,
                 s_ref, x_ref, rk_ref, tv_ref, cand_ref, sel_ref, *, tt, ec):
    c = pl.program_id(1)

    @pl.when(c == 0)
    def _():
        _router(q_ref, sk_ref, n1_ref, r2_ref, e1_ref, e2_ref, s_ref, x_ref, rk_ref, tv_ref,
                cand_ref, sel_ref, tt=tt)
        acc_ref[...] = jnp.zeros_like(acc_ref)

    hid = lax.dot_general(u_ref[...], xn_ref[...], _NT, preferred_element_type=F32)
    act = (0.5 * hid * (1.0 + _erf(hid * (2.0 ** -0.5)))).astype(BF16)
    groups = ec // PEER_KEYS
    ntile = PEER_KEYS // BF16_ROWS
    tiles = []
    for r in range(groups):
        i = c * groups + r
        w = [None] * ntile
        for h in range(PEER_HEADS):
            n1 = jnp.broadcast_to(n1_ref[h, pl.ds(i, 1), :], (BF16_ROWS, tt)).astype(BF16)
            e1 = jnp.broadcast_to(e1_ref[h, pl.ds(i, 1), :], (BF16_ROWS, tt)).astype(BF16)
            for k in range(ntile):
                rows = slice(k * BF16_ROWS, (k + 1) * BF16_ROWS)
                term = jnp.where(r2_ref[h, rows, :] < n1, e1 * e2_ref[h, rows, :], 0.0)
                w[k] = term if w[k] is None else w[k] + term
        base = r * PEER_KEYS
        tiles += [w[k] * act[base + k * BF16_ROWS:base + (k + 1) * BF16_ROWS] for k in range(ntile)]
    a_t = jnp.concatenate(tiles, axis=0)
    acc_ref[...] += jnp.dot(vt_ref[...], a_t, preferred_element_type=F32)

    @pl.when(c == pl.num_programs(1) - 1)
    def _():
        o_ref[...] = acc_ref[...].T


def _peer(xn, q_hm, sk, u, vt, *, tt, ec):
    t, d = xn.shape
    ne = u.shape[0]
    nslab = q_hm.shape[0]
    head_arr = pltpu.VMEM((PEER_HEADS, PEER_KEYS, tt), F32)
    head_arr16 = pltpu.VMEM((PEER_HEADS, PEER_KEYS, tt), BF16)
    return pl.pallas_call(
        functools.partial(_peer_kernel, tt=tt, ec=ec),
        grid=(t // tt, ne // ec),
        in_specs=[
            pl.BlockSpec((tt, d), lambda i, c: (i, 0)),
            pl.BlockSpec((nslab, tt, LANE), lambda i, c: (0, i, 0)),
            _resident((nslab, PEER_KEYS, LANE), lambda i, c: (0, 0, 0)),
            pl.BlockSpec((ec, d), lambda i, c: (c, 0)),
            pl.BlockSpec((d, ec), lambda i, c: (0, c)),
        ],
        out_specs=pl.BlockSpec((tt, d), lambda i, c: (i, 0)),
        out_shape=jax.ShapeDtypeStruct((t, d), F32),
        scratch_shapes=[
            head_arr, head_arr16, head_arr, head_arr16,
            pltpu.VMEM((d, tt), F32),
            pltpu.VMEM((2, PEER_KEYS, tt), F32),
            pltpu.VMEM((PEER_KEYS, tt), F32),
            pltpu.VMEM((2, PEER_KEYS, tt), F32),
            pltpu.VMEM((2, PEER_TOPK, tt), F32),
            pltpu.VMEM((80, tt), F32),
            pltpu.VMEM((80, tt), F32),
        ],
        compiler_params=_cparams(("arbitrary", "arbitrary")),
        name="peer",
    )(xn, q_hm, sk, u, vt)


def _ple_kernel(h1_ref, po_ref, p_ref, g_ref, wg_ref, wp_ref, gf_ref, o_ref, *, final):
    h2 = h1_ref[...] + po_ref[...]
    xn = _rms(h2, g_ref[...]).astype(BF16)
    gate = _sigmoid(jnp.dot(xn, wg_ref[...], preferred_element_type=F32))
    h3 = h2 + gate * jnp.dot(p_ref[...].astype(BF16), wp_ref[...], preferred_element_type=F32)
    if final:
        h3 = _rms(h3, gf_ref[...])
    o_ref[...] = h3


def _ple(h1, peer_out, p, g_ple, w_gate, w_ple, g_final, *, tm, final):
    t, d = h1.shape
    pd = p.shape[1]
    return pl.pallas_call(
        functools.partial(_ple_kernel, final=final),
        grid=(t // tm,),
        in_specs=[
            pl.BlockSpec((tm, d), lambda i: (i, 0)),
            pl.BlockSpec((tm, d), lambda i: (i, 0)),
            pl.BlockSpec((tm, pd), lambda i: (i, 0)),
            pl.BlockSpec((1, d), lambda i: (0, 0)),
            _resident((d, d), lambda i: (0, 0)),
            _resident((pd, d), lambda i: (0, 0)),
            pl.BlockSpec((1, d), lambda i: (0, 0)),
        ],
        out_specs=pl.BlockSpec((tm, d), lambda i: (i, 0)),
        out_shape=jax.ShapeDtypeStruct((t, d), F32),
        compiler_params=_cparams(("arbitrary",)),
        name="ple",
    )(h1, peer_out, p, g_ple, w_gate, w_ple, g_final)


def _pick(n, pref):
    return pref if n % pref == 0 else n


def _layer(h, b, s, p_i, norm_mix_g, w_in, b_forget, w_branch_sb, w_branch_fox, w_out,
           norm_ffn_g, w_query, sub_keys, expert_u, expert_v, norm_ple_g, w_ple, w_ple_gate,
           final_norm_g, final):
    t, d = h.shape
    n_qkv = 6 * WIDTH
    w_main = jnp.concatenate([w_in[:, :n_qkv], w_in[:, n_qkv + N_HEADS:]], axis=1).astype(BF16)
    w_f = jnp.pad(w_in[:, n_qkv:n_qkv + N_HEADS], ((0, 0), (0, LANE - N_HEADS))).astype(BF16)

    proj, flog = _inproj(h, norm_mix_g.reshape(1, d), w_main, w_f, tm=_pick(t, 512))
    proj3 = proj.reshape(b, s, proj.shape[1])

    flog_t = flog[:, :N_HEADS].reshape(b, s, N_HEADS).transpose(0, 2, 1)
    fcum = _fcum(flog_t, b_forget.reshape(N_HEADS, 1).astype(F32))
    tq = _pick(s, 512)
    o_fx = _fox(proj3, fcum.reshape(b, N_HEADS, 1, s), tq=tq)
    o_sb = _sb(proj3, tq=tq)

    h1, xn = _merge(o_sb.reshape(t, WIDTH), o_fx.reshape(t, WIDTH), proj, h,
                    w_branch_sb.astype(BF16), w_branch_fox.astype(BF16), w_out.astype(BF16),
                    norm_ffn_g.reshape(1, d), tm=_pick(t, 256))

    q_hm = _qproj(xn, w_query.astype(BF16), tm=_pick(t, 512))
    sk = sub_keys.reshape(2 * PEER_HEADS, PEER_KEYS, sub_keys.shape[-1]).astype(BF16)
    peer_out = _peer(xn, q_hm, sk, expert_u.astype(BF16), expert_v.T.astype(BF16),
                     tt=_pick(t, 512), ec=512)

    return _ple(h1, peer_out, p_i, norm_ple_g.reshape(1, d), w_ple_gate.astype(BF16),
                w_ple.astype(BF16), final_norm_g.reshape(1, d), tm=_pick(t, 256), final=final)


def kernel(x, p, norm_mix_g, w_in, b_forget, w_branch_sb, w_branch_fox, w_out, norm_ffn_g,
           w_query, sub_keys, expert_u, expert_v, norm_ple_g, w_ple, w_ple_gate, final_norm_g):
    b, s, d = x.shape
    depth = p.shape[0]
    h = x.reshape(b * s, d)
    for i in range(depth):
        h = _layer(h, b, s, p[i].reshape(b * s, -1), norm_mix_g[i], w_in[i], b_forget[i],
                   w_branch_sb[i], w_branch_fox[i], w_out[i], norm_ffn_g[i], w_query[i],
                   sub_keys[i], expert_u[i], expert_v[i], norm_ple_g[i], w_ple[i], w_ple_gate[i],
                   final_norm_g, final=(i == depth - 1))
    return h.reshape(b, s, d)
```

```python
import functools

import jax
import jax.numpy as jnp
from jax import lax
from jax.experimental import pallas as pl
from jax.experimental.pallas import tpu as pltpu

EPS = 1e-6
HEAD_DIM = 128
N_HEADS = 8
WIDTH = N_HEADS * HEAD_DIM
PEER_HEADS = 8
PEER_KEYS = 128
PEER_TOPK = 16
SCALE = HEAD_DIM ** -0.5
LANE = 128
BF16_ROWS = 16
V7X_VMEM_LIMIT_BYTES = 56 * 1024 * 1024
NEG_BIG = -1e30
SB_DEAD_LOG = -112.0

F32 = jnp.float32
BF16 = jnp.bfloat16
_NT = (((1,), (1,)), ((), ()))


def _cparams(semantics):
    return pltpu.CompilerParams(dimension_semantics=semantics,
                                vmem_limit_bytes=V7X_VMEM_LIMIT_BYTES)


def _resident(block_shape, index_map):
    return pl.BlockSpec(block_shape, index_map, pipeline_mode=pl.Buffered(1))


def _rms(x, g):
    return x * lax.rsqrt(jnp.mean(x * x, axis=-1, keepdims=True) + EPS) * g


def _sigmoid(x):
    return 1.0 / (1.0 + jnp.exp(-x))


def _softplus(x):
    return jnp.maximum(x, 0.0) + jnp.log1p(jnp.exp(-jnp.abs(x)))


def _inproj_kernel(h_ref, g_ref, w_ref, wf_ref, o_ref, f_ref, xn_ref):
    j = pl.program_id(1)

    @pl.when(j == 0)
    def _():
        xn = _rms(h_ref[...], g_ref[...]).astype(BF16)
        xn_ref[...] = xn
        f_ref[...] = jnp.dot(xn, wf_ref[...], preferred_element_type=F32)

    acc = jnp.dot(xn_ref[...], w_ref[...], preferred_element_type=F32)
    is_q = jnp.logical_or(j == 0, j == 3)
    is_gate = j >= 6

    @pl.when(is_gate)
    def _():
        o_ref[...] = _sigmoid(acc).astype(o_ref.dtype)

    @pl.when(jnp.logical_not(is_gate))
    def _():
        o_ref[...] = (acc * jnp.where(is_q, SCALE, 1.0)).astype(o_ref.dtype)


def _inproj(h, g, w_main, w_f, *, tm):
    t, d = h.shape
    n = w_main.shape[1]
    tn = WIDTH
    return pl.pallas_call(
        _inproj_kernel,
        grid=(t // tm, n // tn),
        in_specs=[
            pl.BlockSpec((tm, d), lambda i, j: (i, 0)),
            pl.BlockSpec((1, d), lambda i, j: (0, 0)),
            pl.BlockSpec((d, tn), lambda i, j: (0, j)),
            pl.BlockSpec((d, LANE), lambda i, j: (0, 0)),
        ],
        out_specs=[
            pl.BlockSpec((tm, tn), lambda i, j: (i, j)),
            pl.BlockSpec((tm, LANE), lambda i, j: (i, 0)),
        ],
        out_shape=[
            jax.ShapeDtypeStruct((t, n), BF16),
            jax.ShapeDtypeStruct((t, LANE), F32),
        ],
        scratch_shapes=[pltpu.VMEM((tm, d), BF16)],
        compiler_params=_cparams(("arbitrary", "arbitrary")),
        name="inproj",
    )(h, g, w_main, w_f)


def _split3(x):
    hi = x.astype(BF16)
    r = x - hi.astype(F32)
    mid = r.astype(BF16)
    lo = (r - mid.astype(F32)).astype(BF16)
    return hi, mid, lo


def _fcum_kernel(x_ref, b_ref, o_ref, *, chunk):
    s_len = x_ref.shape[-1]
    x = x_ref[...] + b_ref[...]
    lf = -_softplus(-x)
    row = lax.broadcasted_iota(jnp.int32, (chunk, chunk), 0)
    col = lax.broadcasted_iota(jnp.int32, (chunk, chunk), 1)
    tri = jnp.where(row <= col, 1.0, 0.0).astype(BF16)
    carry = jnp.zeros((x.shape[0], 1), F32)
    for c in range(s_len // chunk):
        hi, mid, lo = _split3(lf[:, c * chunk:(c + 1) * chunk])
        cs = (jnp.dot(hi, tri, preferred_element_type=F32)
              + jnp.dot(mid, tri, preferred_element_type=F32)
              + jnp.dot(lo, tri, preferred_element_type=F32)) + carry
        o_ref[:, c * chunk:(c + 1) * chunk] = cs
        carry = cs[:, chunk - 1:chunk]


def _fcum(flog_t, b_col):
    b, nh, s = flog_t.shape
    return pl.pallas_call(
        functools.partial(_fcum_kernel, chunk=min(512, s)),
        grid=(b,),
        in_specs=[
            pl.BlockSpec((None, nh, s), lambda i: (i, 0, 0)),
            pl.BlockSpec((nh, 1), lambda i: (0, 0)),
        ],
        out_specs=pl.BlockSpec((None, nh, s), lambda i: (i, 0, 0)),
        out_shape=jax.ShapeDtypeStruct((b, nh, s), F32),
        compiler_params=_cparams(("arbitrary",)),
        name="forget_cumsum",
    )(flog_t, b_col)


def _fox_kernel(q_ref, k_ref, v_ref, fq_ref, fk_ref, o_ref, m_ref, l_ref, acc_ref, *, tq, hpg):
    qi = pl.program_id(2)
    row = lax.broadcasted_iota(jnp.int32, (tq, tq), 0)
    col = lax.broadcasted_iota(jnp.int32, (tq, tq), 1)
    heads = range(hpg)
    cols = [slice(hh * HEAD_DIM, (hh + 1) * HEAD_DIM) for hh in heads]
    q = [q_ref[:, cols[hh]] for hh in heads]
    fq = [jnp.sum(jnp.where(row == col, fq_ref[hh], 0.0), axis=1, keepdims=True) for hh in heads]

    def scores(hh, j):
        ks = pl.multiple_of(j * tq, tq)
        k = k_ref[pl.ds(ks, tq), cols[hh]]
        s = lax.dot_general(q[hh], k, _NT, preferred_element_type=F32)
        return s + fq[hh] - fk_ref[hh, j], v_ref[pl.ds(ks, tq), cols[hh]]

    for hh in heads:
        s, v = scores(hh, qi)
        s = jnp.where(col <= row, s, NEG_BIG)
        m = jnp.max(s, axis=1, keepdims=True)
        p = jnp.exp(s - m)
        m_ref[hh] = m
        l_ref[hh] = jnp.sum(p, axis=1, keepdims=True)
        acc_ref[hh] = jnp.dot(p.astype(BF16), v, preferred_element_type=F32)

    def body(j, carry):
        state = [(m_ref[hh], l_ref[hh], acc_ref[hh]) for hh in heads]
        new = []
        for hh in heads:
            s, v = scores(hh, j)
            m_prev, l_prev, acc_prev = state[hh]
            m_new = jnp.maximum(m_prev, jnp.max(s, axis=1, keepdims=True))
            alpha = jnp.exp(m_prev - m_new)
            p = jnp.exp(s - m_new)
            new.append((m_new, alpha * l_prev + jnp.sum(p, axis=1, keepdims=True),
                        alpha * acc_prev + jnp.dot(p.astype(BF16), v, preferred_element_type=F32)))
        for hh in heads:
            m_ref[hh], l_ref[hh], acc_ref[hh] = new[hh]
        return carry

    lax.fori_loop(0, qi, body, 0)
    for hh in heads:
        o_ref[:, cols[hh]] = (acc_ref[hh] / l_ref[hh]).astype(o_ref.dtype)


def _fox(proj3, fcum4, *, tq, hpg=2):
    b, s, _ = proj3.shape
    nq = s // tq
    ng = N_HEADS // hpg
    wb = hpg * HEAD_DIM
    qb, kb, vb = 3 * ng, 4 * ng, 5 * ng
    fk5 = fcum4.reshape(b, N_HEADS, nq, 1, tq)
    return pl.pallas_call(
        functools.partial(_fox_kernel, tq=tq, hpg=hpg),
        grid=(b, ng, nq),
        in_specs=[
            pl.BlockSpec((None, tq, wb), lambda bi, g, qi: (bi, qi, qb + g)),
            pl.BlockSpec((None, s, wb), lambda bi, g, qi: (bi, 0, kb + g)),
            pl.BlockSpec((None, s, wb), lambda bi, g, qi: (bi, 0, vb + g)),
            pl.BlockSpec((None, hpg, 1, tq), lambda bi, g, qi: (bi, g, 0, qi)),
            pl.BlockSpec((None, hpg, nq, 1, tq), lambda bi, g, qi: (bi, g, 0, 0, 0)),
        ],
        out_specs=pl.BlockSpec((None, tq, wb), lambda bi, g, qi: (bi, qi, g)),
        out_shape=jax.ShapeDtypeStruct((b, s, WIDTH), BF16),
        scratch_shapes=[
            pltpu.VMEM((hpg, tq, 1), F32),
            pltpu.VMEM((hpg, tq, 1), F32),
            pltpu.VMEM((hpg, tq, HEAD_DIM), F32),
        ],
        compiler_params=_cparams(("arbitrary", "arbitrary", "arbitrary")),
        name="fox_attention",
    )(proj3, proj3, proj3, fcum4, fk5)


def _sb_kernel(q_ref, k_ref, v_ref, o_ref, acc_ref, c_ref, *, tq):
    qi = pl.program_id(2)
    q = q_ref[...]
    nsub = tq // LANE
    r128 = lax.broadcasted_iota(jnp.int32, (LANE, LANE), 0)
    c128 = lax.broadcasted_iota(jnp.int32, (LANE, LANE), 1)
    m_strict = jnp.where(r128 > c128, 1.0, 0.0).astype(BF16)
    m2 = jnp.concatenate([m_strict, m_strict], axis=0)
    row = lax.broadcasted_iota(jnp.int32, (tq, tq), 0)
    col = lax.broadcasted_iota(jnp.int32, (tq, tq), 1)

    def window(w, masked):
        ks = pl.multiple_of(w * tq, tq)
        k = k_ref[pl.ds(ks, tq), :]
        v = v_ref[pl.ds(ks, tq), :]
        z = lax.dot_general(q, k, _NT, preferred_element_type=F32)
        sp = _softplus(z)
        logb = z - sp
        lm = -sp
        if masked:
            valid = col < row
            lm = jnp.where(valid, lm, 0.0)
        run = c_ref[...]
        parts = [None] * nsub
        for sb in reversed(range(nsub)):
            sl = slice(sb * LANE, (sb + 1) * LANE)
            lb = lm[:, sl]
            hi = lb.astype(BF16)
            lo = (lb - hi.astype(F32)).astype(BF16)
            after = jnp.dot(jnp.concatenate([hi, lo], axis=1), m2, preferred_element_type=F32)
            a = jnp.exp(logb[:, sl] + after + run)
            if masked:
                a = jnp.where(valid[:, sl], a, 0.0)
            parts[sb] = a.astype(BF16)
            run = run + after[:, 0:1] + lb[:, 0:1]
        acc_ref[...] += jnp.dot(jnp.concatenate(parts, axis=1), v, preferred_element_type=F32)
        c_ref[...] = run
        return jnp.max(run)

    acc_ref[...] = jnp.zeros_like(acc_ref)
    c_ref[...] = jnp.zeros_like(c_ref)
    cmax0 = window(qi, True)

    def cond(carry):
        w, cmax = carry
        return jnp.logical_and(w >= 0, cmax > SB_DEAD_LOG)

    def body(carry):
        w, _ = carry
        return w - 1, window(w, False)

    lax.while_loop(cond, body, (qi - 1, cmax0))
    o_ref[...] = acc_ref[...].astype(o_ref.dtype)


def _sb(proj3, *, tq):
    b, s, _ = proj3.shape
    nq = s // tq
    qb, kb, vb = 0, N_HEADS, 2 * N_HEADS
    return pl.pallas_call(
        functools.partial(_sb_kernel, tq=tq),
        grid=(b, N_HEADS, nq),
        in_specs=[
            pl.BlockSpec((None, tq, HEAD_DIM), lambda bi, h, qi: (bi, qi, qb + h)),
            pl.BlockSpec((None, s, HEAD_DIM), lambda bi, h, qi: (bi, 0, kb + h)),
            pl.BlockSpec((None, s, HEAD_DIM), lambda bi, h, qi: (bi, 0, vb + h)),
        ],
        out_specs=pl.BlockSpec((None, tq, HEAD_DIM), lambda bi, h, qi: (bi, qi, h)),
        out_shape=jax.ShapeDtypeStruct((b, s, WIDTH), BF16),
        scratch_shapes=[
            pltpu.VMEM((tq, HEAD_DIM), F32),
            pltpu.VMEM((tq, 1), F32),
        ],
        compiler_params=_cparams(("arbitrary", "arbitrary", "arbitrary")),
        name="sb_attention",
    )(proj3, proj3, proj3)


def _merge_kernel(osb_ref, ofx_ref, gsb_ref, gfx_ref, h_ref, wsb_ref, wfx_ref, wout_ref, gn_ref,
                  h1_ref, xn_ref):
    ysb = jnp.dot(osb_ref[...], wsb_ref[...], preferred_element_type=F32)
    yfx = jnp.dot(ofx_ref[...], wfx_ref[...], preferred_element_type=F32)
    merged = gsb_ref[...].astype(F32) * ysb + gfx_ref[...].astype(F32) * yfx
    h1 = h_ref[...] + jnp.dot(merged.astype(BF16), wout_ref[...], preferred_element_type=F32)
    h1_ref[...] = h1
    xn_ref[...] = _rms(h1, gn_ref[...]).astype(BF16)


def _merge(o_sb, o_fx, proj, h, w_sb, w_fx, w_out, g_ffn, *, tm):
    t, d = h.shape
    gate_blk = 6 * WIDTH // d
    return pl.pallas_call(
        _merge_kernel,
        grid=(t // tm,),
        in_specs=[
            pl.BlockSpec((tm, WIDTH), lambda i: (i, 0)),
            pl.BlockSpec((tm, WIDTH), lambda i: (i, 0)),
            pl.BlockSpec((tm, d), lambda i: (i, gate_blk)),
            pl.BlockSpec((tm, d), lambda i: (i, gate_blk + 1)),
            pl.BlockSpec((tm, d), lambda i: (i, 0)),
            _resident((WIDTH, d), lambda i: (0, 0)),
            _resident((WIDTH, d), lambda i: (0, 0)),
            _resident((d, d), lambda i: (0, 0)),
            pl.BlockSpec((1, d), lambda i: (0, 0)),
        ],
        out_specs=[
            pl.BlockSpec((tm, d), lambda i: (i, 0)),
            pl.BlockSpec((tm, d), lambda i: (i, 0)),
        ],
        out_shape=[
            jax.ShapeDtypeStruct((t, d), F32),
            jax.ShapeDtypeStruct((t, d), BF16),
        ],
        compiler_params=_cparams(("arbitrary",)),
        name="merge_outproj",
    )(o_sb, o_fx, proj, proj, h, w_sb, w_fx, w_out, g_ffn)


def _qproj_kernel(x_ref, w_ref, o_ref):
    acc = jnp.dot(x_ref[...], w_ref[...], preferred_element_type=F32)
    for s in range(o_ref.shape[0]):
        o_ref[s] = acc[:, s * LANE:(s + 1) * LANE].astype(o_ref.dtype)


def _qproj(xn, w_q, *, tm):
    t, d = xn.shape
    n = w_q.shape[1]
    return pl.pallas_call(
        _qproj_kernel,
        grid=(t // tm,),
        in_specs=[
            pl.BlockSpec((tm, d), lambda i: (i, 0)),
            _resident((d, n), lambda i: (0, 0)),
        ],
        out_specs=pl.BlockSpec((n // LANE, tm, LANE), lambda i: (0, i, 0)),
        out_shape=jax.ShapeDtypeStruct((n // LANE, t, LANE), BF16),
        compiler_params=_cparams(("arbitrary",)),
        name="peer_query",
    )(xn, w_q)


def _erf(x):
    return lax.erf(x)


def _extract16(x_ref, mark_ref, order_f, on_round, *, break_ties):
    mark_ref[...] = jnp.full(mark_ref.shape, float(PEER_TOPK), F32)

    def rnd(a, carry):
        x = x_ref[...]
        m = jnp.max(x, axis=0, keepdims=True)
        sel = x == m
        if break_ties:
            first = jnp.min(jnp.where(sel, order_f, 1e9), axis=0, keepdims=True)
            sel = order_f == first
        mark_ref[...] = jnp.where(sel, lax.convert_element_type(a, F32), mark_ref[...])
        x_ref[...] = jnp.where(sel, -jnp.inf, x)
        on_round(a, m)
        return carry

    lax.fori_loop(0, PEER_TOPK, rnd, 0)


def _top16_exact(load_x, x_ref, mark_ref, order_f, on_round):
    x_ref[...] = load_x()
    _extract16(x_ref, mark_ref, order_f, on_round, break_ties=False)
    taken = jnp.sum(jnp.where(mark_ref[...] < float(PEER_TOPK), 1.0, 0.0), axis=0, keepdims=True)
    clean = jnp.logical_and(jnp.max(taken) == float(PEER_TOPK), jnp.min(taken) == float(PEER_TOPK))

    @pl.when(jnp.logical_not(clean))
    def _():
        x_ref[...] = load_x()
        _extract16(x_ref, mark_ref, order_f, on_round, break_ties=True)


def _pair_candidates(tv_ref):
    s2lo = tv_ref[1, 0:8, :]
    groups = [tv_ref[0, a:a + 1, :] + s2lo for a in range(8)]
    groups.append(tv_ref[0, 0:1, :] + tv_ref[1, 8:16, :])
    groups.append(tv_ref[0, 8:16, :] + tv_ref[1, 0:1, :])
    return jnp.concatenate(groups, axis=0)


def _router(q_ref, sk_ref, n1_ref, r2_ref, e1_ref, e2_ref, s_ref, x_ref, rk_ref, tv_ref,
            cand_ref, sel_ref, *, tt):
    iota_f = lax.broadcasted_iota(jnp.int32, (PEER_KEYS, tt), 0).astype(F32)
    r80 = lax.broadcasted_iota(jnp.int32, (80, tt), 0)
    pos = jnp.where(r80 < 64, (r80 >> 3) * 16 + (r80 & 7),
                    jnp.where(r80 < 72, r80 - 56, (r80 - 64) * 16)).astype(F32)

    def head(h, carry):
        for half in range(2):
            st = lax.dot_general(sk_ref[2 * h + half], q_ref[2 * h + half], _NT,
                                 preferred_element_type=F32)
            s_ref[half] = st

            def keep_top_value(a, m, half=half):
                tv_ref[half, pl.ds(a, 1), :] = m

            _top16_exact(lambda half=half: s_ref[half], x_ref, rk_ref.at[half], iota_f,
                         keep_top_value)

        _top16_exact(lambda: _pair_candidates(tv_ref), cand_ref, sel_ref, pos, lambda a, m: None)
        selm = jnp.where(sel_ref[...] < float(PEER_TOPK), 1.0, 0.0)
        m1 = tv_ref[0, 0:1, :]
        m2 = tv_ref[1, 0:1, :]
        zsum = jnp.sum(jnp.where(selm > 0.0, jnp.exp(_pair_candidates(tv_ref) - (m1 + m2)), 0.0),
                       axis=0, keepdims=True)
        rk1 = rk_ref[0]
        n1 = jnp.zeros((PEER_KEYS, tt), F32)
        for a in range(PEER_TOPK):
            if a < 8:
                na = jnp.sum(selm[a * 8:(a + 1) * 8], axis=0, keepdims=True)
                if a == 0:
                    na = na + jnp.sum(selm[64:72], axis=0, keepdims=True)
            else:
                na = selm[64 + a:65 + a]
            n1 = jnp.where(rk1 == float(a), na, n1)
        n1_ref[h] = n1
        r2_ref[h] = rk_ref[1].astype(BF16)
        e1_ref[h] = jnp.exp(s_ref[0] - m1) * (1.0 / zsum)
        e2_ref[h] = jnp.exp(s_ref[1] - m2).astype(BF16)
        return carry

    lax.fori_loop(0, PEER_HEADS, head, 0)


def _peer_kernel(xn_ref, q_ref, sk_ref, u_ref, vt_ref, o_ref,
                 n1_ref, r2_ref, e1_ref, e2_ref, acc_ref,
                 s_ref, x_ref, rk_ref, tv_ref, cand_ref, sel_ref, *, tt, ec):
    c = pl.program_id(1)

    @pl.when(c == 0)
    def _():
        _router(q_ref, sk_ref, n1_ref, r2_ref, e1_ref, e2_ref, s_ref, x_ref, rk_ref, tv_ref,
                cand_ref, sel_ref, tt=tt)
        acc_ref[...] = jnp.zeros_like(acc_ref)

    hid = lax.dot_general(u_ref[...], xn_ref[...], _NT, preferred_element_type=F32)
    act = (0.5 * hid * (1.0 + _erf(hid * (2.0 ** -0.5)))).astype(BF16)
    groups = ec // PEER_KEYS
    ntile = PEER_KEYS // BF16_ROWS
    tiles = []
    for r in range(groups):
        i = c * groups + r
        w = [None] * ntile
        for h in range(PEER_HEADS):
            n1 = jnp.broadcast_to(n1_ref[h, pl.ds(i, 1), :], (BF16_ROWS, tt)).astype(BF16)
            e1 = jnp.broadcast_to(e1_ref[h, pl.ds(i, 1), :], (BF16_ROWS, tt)).astype(BF16)
            for k in range(ntile):
                rows = slice(k * BF16_ROWS, (k + 1) * BF16_ROWS)
                term = jnp.where(r2_ref[h, rows, :] < n1, e1 * e2_ref[h, rows, :], 0.0)
                w[k] = term if w[k] is None else w[k] + term
        base = r * PEER_KEYS
        tiles += [w[k] * act[base + k * BF16_ROWS:base + (k + 1) * BF16_ROWS] for k in range(ntile)]
    a_t = jnp.concatenate(tiles, axis=0)
    acc_ref[...] += jnp.dot(vt_ref[...], a_t, preferred_element_type=F32)

    @pl.when(c == pl.num_programs(1) - 1)
    def _():
        o_ref[...] = acc_ref[...].T


def _peer(xn, q_hm, sk, u, vt, *, tt, ec):
    t, d = xn.shape
    ne = u.shape[0]
    nslab = q_hm.shape[0]
    head_arr = pltpu.VMEM((PEER_HEADS, PEER_KEYS, tt), F32)
    head_arr16 = pltpu.VMEM((PEER_HEADS, PEER_KEYS, tt), BF16)
    return pl.pallas_call(
        functools.partial(_peer_kernel, tt=tt, ec=ec),
        grid=(t // tt, ne // ec),
        in_specs=[
            pl.BlockSpec((tt, d), lambda i, c: (i, 0)),
            pl.BlockSpec((nslab, tt, LANE), lambda i, c: (0, i, 0)),
            _resident((nslab, PEER_KEYS, LANE), lambda i, c: (0, 0, 0)),
            pl.BlockSpec((ec, d), lambda i, c: (c, 0)),
            pl.BlockSpec((d, ec), lambda i, c: (0, c)),
        ],
        out_specs=pl.BlockSpec((tt, d), lambda i, c: (i, 0)),
        out_shape=jax.ShapeDtypeStruct((t, d), F32),
        scratch_shapes=[
            head_arr, head_arr16, head_arr, head_arr16,
            pltpu.VMEM((d, tt), F32),
            pltpu.VMEM((2, PEER_KEYS, tt), F32),
            pltpu.VMEM((PEER_KEYS, tt), F32),
            pltpu.VMEM((2, PEER_KEYS, tt), F32),
            pltpu.VMEM((2, PEER_TOPK, tt), F32),
            pltpu.VMEM((80, tt), F32),
            pltpu.VMEM((80, tt), F32),
        ],
        compiler_params=_cparams(("arbitrary", "arbitrary")),
        name="peer",
    )(xn, q_hm, sk, u, vt)


def _ple_kernel(h1_ref, po_ref, p_ref, g_ref, wg_ref, wp_ref, gf_ref, o_ref, *, final):
    h2 = h1_ref[...] + po_ref[...]
    xn = _rms(h2, g_ref[...]).astype(BF16)
    gate = _sigmoid(jnp.dot(xn, wg_ref[...], preferred_element_type=F32))
    h3 = h2 + gate * jnp.dot(p_ref[...].astype(BF16), wp_ref[...], preferred_element_type=F32)
    if final:
        h3 = _rms(h3, gf_ref[...])
    o_ref[...] = h3


def _ple(h1, peer_out, p, g_ple, w_gate, w_ple, g_final, *, tm, final):
    t, d = h1.shape
    pd = p.shape[1]
    return pl.pallas_call(
        functools.partial(_ple_kernel, final=final),
        grid=(t // tm,),
        in_specs=[
            pl.BlockSpec((tm, d), lambda i: (i, 0)),
            pl.BlockSpec((tm, d), lambda i: (i, 0)),
            pl.BlockSpec((tm, pd), lambda i: (i, 0)),
            pl.BlockSpec((1, d), lambda i: (0, 0)),
            _resident((d, d), lambda i: (0, 0)),
            _resident((pd, d), lambda i: (0, 0)),
            pl.BlockSpec((1, d), lambda i: (0, 0)),
        ],
        out_specs=pl.BlockSpec((tm, d), lambda i: (i, 0)),
        out_shape=jax.ShapeDtypeStruct((t, d), F32),
        compiler_params=_cparams(("arbitrary",)),
        name="ple",
    )(h1, peer_out, p, g_ple, w_gate, w_ple, g_final)


def _pick(n, pref):
    return pref if n % pref == 0 else n


def _layer(h, b, s, p_i, norm_mix_g, w_in, b_forget, w_branch_sb, w_branch_fox, w_out,
           norm_ffn_g, w_query, sub_keys, expert_u, expert_v, norm_ple_g, w_ple, w_ple_gate,
           final_norm_g, final):
    t, d = h.shape
    n_qkv = 6 * WIDTH
    w_main = jnp.concatenate([w_in[:, :n_qkv], w_in[:, n_qkv + N_HEADS:]], axis=1).astype(BF16)
    w_f = jnp.pad(w_in[:, n_qkv:n_qkv + N_HEADS], ((0, 0), (0, LANE - N_HEADS))).astype(BF16)

    proj, flog = _inproj(h, norm_mix_g.reshape(1, d), w_main, w_f, tm=_pick(t, 1024))
    proj3 = proj.reshape(b, s, proj.shape[1])

    flog_t = flog[:, :N_HEADS].reshape(b, s, N_HEADS).transpose(0, 2, 1)
    fcum = _fcum(flog_t, b_forget.reshape(N_HEADS, 1).astype(F32))
    tq = _pick(s, 512)
    o_fx = _fox(proj3, fcum.reshape(b, N_HEADS, 1, s), tq=tq)
    o_sb = _sb(proj3, tq=tq)

    h1, xn = _merge(o_sb.reshape(t, WIDTH), o_fx.reshape(t, WIDTH), proj, h,
                    w_branch_sb.astype(BF16), w_branch_fox.astype(BF16), w_out.astype(BF16),
                    norm_ffn_g.reshape(1, d), tm=_pick(t, 256))

    q_hm = _qproj(xn, w_query.astype(BF16), tm=_pick(t, 512))
    sk = sub_keys.reshape(2 * PEER_HEADS, PEER_KEYS, sub_keys.shape[-1]).astype(BF16)
    peer_out = _peer(xn, q_hm, sk, expert_u.astype(BF16), expert_v.T.astype(BF16),
                     tt=_pick(t, 512), ec=1024)

    return _ple(h1, peer_out, p_i, norm_ple_g.reshape(1, d), w_ple_gate.astype(BF16),
                w_ple.astype(BF16), final_norm_g.reshape(1, d), tm=_pick(t, 256), final=final)


def kernel(x, p, norm_mix_g, w_in, b_forget, w_branch_sb, w_branch_fox, w_out, norm_ffn_g,
           w_query, sub_keys, expert_u, expert_v, norm_ple_g, w_ple, w_ple_gate, final_norm_g):
    b, s, d = x.shape
    depth = p.shape[0]
    h = x.reshape(b * s, d)
    for i in range(depth):
        h = _layer(h, b, s, p[i].reshape(b * s, -1), norm_mix_g[i], w_in[i], b_forget[i],
                   w_branch_sb[i], w_branch_fox[i], w_out[i], norm_ffn_g[i], w_query[i],
                   sub_keys[i], expert_u[i], expert_v[i], norm_ple_g[i], w_ple[i], w_ple_gate[i],
                   final_norm_g, final=(i == depth - 1))
    return h.reshape(b, s, d)
```

```python
import functools

import jax
import jax.numpy as jnp
from jax import lax
from jax.experimental import pallas as pl
from jax.experimental.pallas import tpu as pltpu

EPS = 1e-6
HEAD_DIM = 128
N_HEADS = 8
WIDTH = N_HEADS * HEAD_DIM
PEER_HEADS = 8
PEER_KEYS = 128
PEER_TOPK = 16
SCALE = HEAD_DIM ** -0.5
LANE = 128
BF16_ROWS = 16
V7X_VMEM_LIMIT_BYTES = 56 * 1024 * 1024
NEG_BIG = -1e30
SB_DEAD_LOG = -112.0

F32 = jnp.float32
BF16 = jnp.bfloat16
_NT = (((1,), (1,)), ((), ()))


def _cparams(semantics):
    return pltpu.CompilerParams(dimension_semantics=semantics,
                                vmem_limit_bytes=V7X_VMEM_LIMIT_BYTES)


def _resident(block_shape, index_map):
    return pl.BlockSpec(block_shape, index_map, pipeline_mode=pl.Buffered(1))


def _rms(x, g):
    return x * lax.rsqrt(jnp.mean(x * x, axis=-1, keepdims=True) + EPS) * g


def _sigmoid(x):
    return 1.0 / (1.0 + jnp.exp(-x))


def _softplus(x):
    return jnp.maximum(x, 0.0) + jnp.log1p(jnp.exp(-jnp.abs(x)))


def _inproj_kernel(h_ref, g_ref, w_ref, wf_ref, o_ref, f_ref, xn_ref):
    j = pl.program_id(1)

    @pl.when(j == 0)
    def _():
        xn = _rms(h_ref[...], g_ref[...]).astype(BF16)
        xn_ref[...] = xn
        f_ref[...] = jnp.dot(xn, wf_ref[...], preferred_element_type=F32)

    acc = jnp.dot(xn_ref[...], w_ref[...], preferred_element_type=F32)
    is_q = jnp.logical_or(j == 0, j == 3)
    is_gate = j >= 6

    @pl.when(is_gate)
    def _():
        o_ref[...] = _sigmoid(acc).astype(o_ref.dtype)

    @pl.when(jnp.logical_not(is_gate))
    def _():
        o_ref[...] = (acc * jnp.where(is_q, SCALE, 1.0)).astype(o_ref.dtype)


def _inproj(h, g, w_main, w_f, *, tm):
    t, d = h.shape
    n = w_main.shape[1]
    tn = WIDTH
    return pl.pallas_call(
        _inproj_kernel,
        grid=(t // tm, n // tn),
        in_specs=[
            pl.BlockSpec((tm, d), lambda i, j: (i, 0)),
            pl.BlockSpec((1, d), lambda i, j: (0, 0)),
            pl.BlockSpec((d, tn), lambda i, j: (0, j)),
            pl.BlockSpec((d, LANE), lambda i, j: (0, 0)),
        ],
        out_specs=[
            pl.BlockSpec((tm, tn), lambda i, j: (i, j)),
            pl.BlockSpec((tm, LANE), lambda i, j: (i, 0)),
        ],
        out_shape=[
            jax.ShapeDtypeStruct((t, n), BF16),
            jax.ShapeDtypeStruct((t, LANE), F32),
        ],
        scratch_shapes=[pltpu.VMEM((tm, d), BF16)],
        compiler_params=_cparams(("arbitrary", "arbitrary")),
        name="inproj",
    )(h, g, w_main, w_f)


def _split3(x):
    hi = x.astype(BF16)
    r = x - hi.astype(F32)
    mid = r.astype(BF16)
    lo = (r - mid.astype(F32)).astype(BF16)
    return hi, mid, lo


def _fcum_kernel(x_ref, b_ref, o_ref, *, chunk):
    s_len = x_ref.shape[-1]
    x = x_ref[...] + b_ref[...]
    lf = -_softplus(-x)
    row = lax.broadcasted_iota(jnp.int32, (chunk, chunk), 0)
    col = lax.broadcasted_iota(jnp.int32, (chunk, chunk), 1)
    tri = jnp.where(row <= col, 1.0, 0.0).astype(BF16)
    carry = jnp.zeros((x.shape[0], 1), F32)
    for c in range(s_len // chunk):
        hi, mid, lo = _split3(lf[:, c * chunk:(c + 1) * chunk])
        cs = (jnp.dot(hi, tri, preferred_element_type=F32)
              + jnp.dot(mid, tri, preferred_element_type=F32)
              + jnp.dot(lo, tri, preferred_element_type=F32)) + carry
        o_ref[:, c * chunk:(c + 1) * chunk] = cs
        carry = cs[:, chunk - 1:chunk]


def _fcum(flog_t, b_col):
    b, nh, s = flog_t.shape
    return pl.pallas_call(
        functools.partial(_fcum_kernel, chunk=min(512, s)),
        grid=(b,),
        in_specs=[
            pl.BlockSpec((None, nh, s), lambda i: (i, 0, 0)),
            pl.BlockSpec((nh, 1), lambda i: (0, 0)),
        ],
        out_specs=pl.BlockSpec((None, nh, s), lambda i: (i, 0, 0)),
        out_shape=jax.ShapeDtypeStruct((b, nh, s), F32),
        compiler_params=_cparams(("arbitrary",)),
        name="forget_cumsum",
    )(flog_t, b_col)


def _fox_kernel(q_ref, k_ref, v_ref, fq_ref, fk_ref, o_ref, m_ref, l_ref, acc_ref, *, tq, hpg):
    qi = pl.program_id(2)
    row = lax.broadcasted_iota(jnp.int32, (tq, tq), 0)
    col = lax.broadcasted_iota(jnp.int32, (tq, tq), 1)
    heads = range(hpg)
    cols = [slice(hh * HEAD_DIM, (hh + 1) * HEAD_DIM) for hh in heads]
    q = [q_ref[:, cols[hh]] for hh in heads]
    fq = [jnp.sum(jnp.where(row == col, fq_ref[hh], 0.0), axis=1, keepdims=True) for hh in heads]

    def scores(hh, j):
        ks = pl.multiple_of(j * tq, tq)
        k = k_ref[pl.ds(ks, tq), cols[hh]]
        s = lax.dot_general(q[hh], k, _NT, preferred_element_type=F32)
        return s + fq[hh] - fk_ref[hh, j], v_ref[pl.ds(ks, tq), cols[hh]]

    for hh in heads:
        s, v = scores(hh, qi)
        s = jnp.where(col <= row, s, NEG_BIG)
        m = jnp.max(s, axis=1, keepdims=True)
        p = jnp.exp(s - m)
        m_ref[hh] = m
        l_ref[hh] = jnp.sum(p, axis=1, keepdims=True)
        acc_ref[hh] = jnp.dot(p.astype(BF16), v, preferred_element_type=F32)

    def body(j, carry):
        state = [(m_ref[hh], l_ref[hh], acc_ref[hh]) for hh in heads]
        new = []
        for hh in heads:
            s, v = scores(hh, j)
            m_prev, l_prev, acc_prev = state[hh]
            m_new = jnp.maximum(m_prev, jnp.max(s, axis=1, keepdims=True))
            alpha = jnp.exp(m_prev - m_new)
            p = jnp.exp(s - m_new)
            new.append((m_new, alpha * l_prev + jnp.sum(p, axis=1, keepdims=True),
                        alpha * acc_prev + jnp.dot(p.astype(BF16), v, preferred_element_type=F32)))
        for hh in heads:
            m_ref[hh], l_ref[hh], acc_ref[hh] = new[hh]
        return carry

    lax.fori_loop(0, qi, body, 0)
    for hh in heads:
        o_ref[:, cols[hh]] = (acc_ref[hh] / l_ref[hh]).astype(o_ref.dtype)


def _fox(proj3, fcum4, *, tq, hpg=2):
    b, s, _ = proj3.shape
    nq = s // tq
    ng = N_HEADS // hpg
    wb = hpg * HEAD_DIM
    qb, kb, vb = 3 * ng, 4 * ng, 5 * ng
    fk5 = fcum4.reshape(b, N_HEADS, nq, 1, tq)
    return pl.pallas_call(
        functools.partial(_fox_kernel, tq=tq, hpg=hpg),
        grid=(b, ng, nq),
        in_specs=[
            pl.BlockSpec((None, tq, wb), lambda bi, g, qi: (bi, qi, qb + g)),
            pl.BlockSpec((None, s, wb), lambda bi, g, qi: (bi, 0, kb + g)),
            pl.BlockSpec((None, s, wb), lambda bi, g, qi: (bi, 0, vb + g)),
            pl.BlockSpec((None, hpg, 1, tq), lambda bi, g, qi: (bi, g, 0, qi)),
            pl.BlockSpec((None, hpg, nq, 1, tq), lambda bi, g, qi: (bi, g, 0, 0, 0)),
        ],
        out_specs=pl.BlockSpec((None, tq, wb), lambda bi, g, qi: (bi, qi, g)),
        out_shape=jax.ShapeDtypeStruct((b, s, WIDTH), BF16),
        scratch_shapes=[
            pltpu.VMEM((hpg, tq, 1), F32),
            pltpu.VMEM((hpg, tq, 1), F32),
            pltpu.VMEM((hpg, tq, HEAD_DIM), F32),
        ],
        compiler_params=_cparams(("arbitrary", "arbitrary", "arbitrary")),
        name="fox_attention",
    )(proj3, proj3, proj3, fcum4, fk5)


def _sb_kernel(q_ref, k_ref, v_ref, o_ref, acc_ref, c_ref, *, tq):
    qi = pl.program_id(2)
    q = q_ref[...]
    nsub = tq // LANE
    r128 = lax.broadcasted_iota(jnp.int32, (LANE, LANE), 0)
    c128 = lax.broadcasted_iota(jnp.int32, (LANE, LANE), 1)
    m_strict = jnp.where(r128 > c128, 1.0, 0.0).astype(BF16)
    m2 = jnp.concatenate([m_strict, m_strict], axis=0)
    row = lax.broadcasted_iota(jnp.int32, (tq, tq), 0)
    col = lax.broadcasted_iota(jnp.int32, (tq, tq), 1)

    def window(w, masked):
        ks = pl.multiple_of(w * tq, tq)
        k = k_ref[pl.ds(ks, tq), :]
        v = v_ref[pl.ds(ks, tq), :]
        z = lax.dot_general(q, k, _NT, preferred_element_type=F32)
        sp = _softplus(z)
        logb = z - sp
        lm = -sp
        if masked:
            valid = col < row
            lm = jnp.where(valid, lm, 0.0)
        run = c_ref[...]
        parts = [None] * nsub
        for sb in reversed(range(nsub)):
            sl = slice(sb * LANE, (sb + 1) * LANE)
            lb = lm[:, sl]
            hi = lb.astype(BF16)
            lo = (lb - hi.astype(F32)).astype(BF16)
            after = jnp.dot(jnp.concatenate([hi, lo], axis=1), m2, preferred_element_type=F32)
            a = jnp.exp(logb[:, sl] + after + run)
            if masked:
                a = jnp.where(valid[:, sl], a, 0.0)
            parts[sb] = a.astype(BF16)
            run = run + after[:, 0:1] + lb[:, 0:1]
        acc_ref[...] += jnp.dot(jnp.concatenate(parts, axis=1), v, preferred_element_type=F32)
        c_ref[...] = run
        return jnp.max(run)

    acc_ref[...] = jnp.zeros_like(acc_ref)
    c_ref[...] = jnp.zeros_like(c_ref)
    cmax0 = window(qi, True)

    def cond(carry):
        w, cmax = carry
        return jnp.logical_and(w >= 0, cmax > SB_DEAD_LOG)

    def body(carry):
        w, _ = carry
        return w - 1, window(w, False)

    lax.while_loop(cond, body, (qi - 1, cmax0))
    o_ref[...] = acc_ref[...].astype(o_ref.dtype)


def _sb(proj3, *, tq):
    b, s, _ = proj3.shape
    nq = s // tq
    qb, kb, vb = 0, N_HEADS, 2 * N_HEADS
    return pl.pallas_call(
        functools.partial(_sb_kernel, tq=tq),
        grid=(b, N_HEADS, nq),
        in_specs=[
            pl.BlockSpec((None, tq, HEAD_DIM), lambda bi, h, qi: (bi, qi, qb + h)),
            pl.BlockSpec((None, s, HEAD_DIM), lambda bi, h, qi: (bi, 0, kb + h)),
            pl.BlockSpec((None, s, HEAD_DIM), lambda bi, h, qi: (bi, 0, vb + h)),
        ],
        out_specs=pl.BlockSpec((None, tq, HEAD_DIM), lambda bi, h, qi: (bi, qi, h)),
        out_shape=jax.ShapeDtypeStruct((b, s, WIDTH), BF16),
        scratch_shapes=[
            pltpu.VMEM((tq, HEAD_DIM), F32),
            pltpu.VMEM((tq, 1), F32),
        ],
        compiler_params=_cparams(("arbitrary", "arbitrary", "arbitrary")),
        name="sb_attention",
    )(proj3, proj3, proj3)


def _merge_kernel(osb_ref, ofx_ref, gsb_ref, gfx_ref, h_ref, wsb_ref, wfx_ref, wout_ref, gn_ref,
                  h1_ref, xn_ref):
    ysb = jnp.dot(osb_ref[...], wsb_ref[...], preferred_element_type=F32)
    yfx = jnp.dot(ofx_ref[...], wfx_ref[...], preferred_element_type=F32)
    merged = gsb_ref[...].astype(F32) * ysb + gfx_ref[...].astype(F32) * yfx
    h1 = h_ref[...] + jnp.dot(merged.astype(BF16), wout_ref[...], preferred_element_type=F32)
    h1_ref[...] = h1
    xn_ref[...] = _rms(h1, gn_ref[...]).astype(BF16)


def _merge(o_sb, o_fx, proj, h, w_sb, w_fx, w_out, g_ffn, *, tm):
    t, d = h.shape
    gate_blk = 6 * WIDTH // d
    return pl.pallas_call(
        _merge_kernel,
        grid=(t // tm,),
        in_specs=[
            pl.BlockSpec((tm, WIDTH), lambda i: (i, 0)),
            pl.BlockSpec((tm, WIDTH), lambda i: (i, 0)),
            pl.BlockSpec((tm, d), lambda i: (i, gate_blk)),
            pl.BlockSpec((tm, d), lambda i: (i, gate_blk + 1)),
            pl.BlockSpec((tm, d), lambda i: (i, 0)),
            _resident((WIDTH, d), lambda i: (0, 0)),
            _resident((WIDTH, d), lambda i: (0, 0)),
            _resident((d, d), lambda i: (0, 0)),
            pl.BlockSpec((1, d), lambda i: (0, 0)),
        ],
        out_specs=[
            pl.BlockSpec((tm, d), lambda i: (i, 0)),
            pl.BlockSpec((tm, d), lambda i: (i, 0)),
        ],
        out_shape=[
            jax.ShapeDtypeStruct((t, d), F32),
            jax.ShapeDtypeStruct((t, d), BF16),
        ],
        compiler_params=_cparams(("arbitrary",)),
        name="merge_outproj",
    )(o_sb, o_fx, proj, proj, h, w_sb, w_fx, w_out, g_ffn)


def _qproj_kernel(x_ref, w_ref, o_ref):
    acc = jnp.dot(x_ref[...], w_ref[...], preferred_element_type=F32)
    for s in range(o_ref.shape[0]):
        o_ref[s] = acc[:, s * LANE:(s + 1) * LANE].astype(o_ref.dtype)


def _qproj(xn, w_q, *, tm):
    t, d = xn.shape
    n = w_q.shape[1]
    return pl.pallas_call(
        _qproj_kernel,
        grid=(t // tm,),
        in_specs=[
            pl.BlockSpec((tm, d), lambda i: (i, 0)),
            _resident((d, n), lambda i: (0, 0)),
        ],
        out_specs=pl.BlockSpec((n // LANE, tm, LANE), lambda i: (0, i, 0)),
        out_shape=jax.ShapeDtypeStruct((n // LANE, t, LANE), BF16),
        compiler_params=_cparams(("arbitrary",)),
        name="peer_query",
    )(xn, w_q)


def _erf(x):
    return lax.erf(x)


def _extract16(x_ref, mark_ref, order_f, on_round, *, break_ties):
    mark_ref[...] = jnp.full(mark_ref.shape, float(PEER_TOPK), F32)

    def rnd(a, carry):
        x = x_ref[...]
        m = jnp.max(x, axis=0, keepdims=True)
        sel = x == m
        if break_ties:
            first = jnp.min(jnp.where(sel, order_f, 1e9), axis=0, keepdims=True)
            sel = order_f == first
        mark_ref[...] = jnp.where(sel, lax.convert_element_type(a, F32), mark_ref[...])
        x_ref[...] = jnp.where(sel, -jnp.inf, x)
        on_round(a, m)
        return carry

    lax.fori_loop(0, PEER_TOPK, rnd, 0)


def _top16_exact(load_x, x_ref, mark_ref, order_f, on_round):
    x_ref[...] = load_x()
    _extract16(x_ref, mark_ref, order_f, on_round, break_ties=False)
    taken = jnp.sum(jnp.where(mark_ref[...] < float(PEER_TOPK), 1.0, 0.0), axis=0, keepdims=True)
    clean = jnp.logical_and(jnp.max(taken) == float(PEER_TOPK), jnp.min(taken) == float(PEER_TOPK))

    @pl.when(jnp.logical_not(clean))
    def _():
        x_ref[...] = load_x()
        _extract16(x_ref, mark_ref, order_f, on_round, break_ties=True)


def _pair_candidates(tv_ref):
    s2lo = tv_ref[1, 0:8, :]
    groups = [tv_ref[0, a:a + 1, :] + s2lo for a in range(8)]
    groups.append(tv_ref[0, 0:1, :] + tv_ref[1, 8:16, :])
    groups.append(tv_ref[0, 8:16, :] + tv_ref[1, 0:1, :])
    return jnp.concatenate(groups, axis=0)


def _router(q_ref, sk_ref, n1_ref, r2_ref, e1_ref, e2_ref, s_ref, x_ref, rk_ref, tv_ref,
            cand_ref, sel_ref, *, tt):
    iota_f = lax.broadcasted_iota(jnp.int32, (PEER_KEYS, tt), 0).astype(F32)
    r80 = lax.broadcasted_iota(jnp.int32, (80, tt), 0)
    pos = jnp.where(r80 < 64, (r80 >> 3) * 16 + (r80 & 7),
                    jnp.where(r80 < 72, r80 - 56, (r80 - 64) * 16)).astype(F32)

    def head(h, carry):
        for half in range(2):
            st = lax.dot_general(sk_ref[2 * h + half], q_ref[2 * h + half], _NT,
                                 preferred_element_type=F32)
            s_ref[half] = st

            def keep_top_value(a, m, half=half):
                tv_ref[half, pl.ds(a, 1), :] = m

            _top16_exact(lambda half=half: s_ref[half], x_ref, rk_ref.at[half], iota_f,
                         keep_top_value)

        _top16_exact(lambda: _pair_candidates(tv_ref), cand_ref, sel_ref, pos, lambda a, m: None)
        selm = jnp.where(sel_ref[...] < float(PEER_TOPK), 1.0, 0.0)
        m1 = tv_ref[0, 0:1, :]
        m2 = tv_ref[1, 0:1, :]
        zsum = jnp.sum(jnp.where(selm > 0.0, jnp.exp(_pair_candidates(tv_ref) - (m1 + m2)), 0.0),
                       axis=0, keepdims=True)
        rk1 = rk_ref[0]
        n1 = jnp.zeros((PEER_KEYS, tt), F32)
        for a in range(PEER_TOPK):
            if a < 8:
                na = jnp.sum(selm[a * 8:(a + 1) * 8], axis=0, keepdims=True)
                if a == 0:
                    na = na + jnp.sum(selm[64:72], axis=0, keepdims=True)
            else:
                na = selm[64 + a:65 + a]
            n1 = jnp.where(rk1 == float(a), na, n1)
        n1_ref[h] = n1
        r2_ref[h] = rk_ref[1].astype(BF16)
        e1_ref[h] = jnp.exp(s_ref[0] - m1) * (1.0 / zsum)
        e2_ref[h] = jnp.exp(s_ref[1] - m2).astype(BF16)
        return carry

    lax.fori_loop(0, PEER_HEADS, head, 0)


def _peer_kernel(xn_ref, q_ref, sk_ref, u_ref, vt_ref, o_ref,
                 n1_ref, r2_ref, e1_ref, e2_ref, acc_ref,
                 s_ref, x_ref, rk_ref, tv_ref, cand_ref, sel_ref, *, tt, ec):
    c = pl.program_id(1)

    @pl.when(c == 0)
    def _():
        _router(q_ref, sk_ref, n1_ref, r2_ref, e1_ref, e2_ref, s_ref, x_ref, rk_ref, tv_ref,
                cand_ref, sel_ref, tt=tt)
        acc_ref[...] = jnp.zeros_like(acc_ref)

    hid = lax.dot_general(u_ref[...], xn_ref[...], _NT, preferred_element_type=F32)
    act = (0.5 * hid * (1.0 + _erf(hid * (2.0 ** -0.5)))).astype(BF16)
    groups = ec // PEER_KEYS
    ntile = PEER_KEYS // BF16_ROWS
    tiles = []
    for r in range(groups):
        i = c * groups + r
        w = [None] * ntile
        for h in range(PEER_HEADS):
            n1 = jnp.broadcast_to(n1_ref[h, pl.ds(i, 1), :], (BF16_ROWS, tt)).astype(BF16)
            e1 = jnp.broadcast_to(e1_ref[h, pl.ds(i, 1), :], (BF16_ROWS, tt)).astype(BF16)
            for k in range(ntile):
                rows = slice(k * BF16_ROWS, (k + 1) * BF16_ROWS)
                term = jnp.where(r2_ref[h, rows, :] < n1, e1 * e2_ref[h, rows, :], 0.0)
                w[k] = term if w[k] is None else w[k] + term
        base = r * PEER_KEYS
        tiles += [w[k] * act[base + k * BF16_ROWS:base + (k + 1) * BF16_ROWS] for k in range(ntile)]
    a_t = jnp.concatenate(tiles, axis=0)
    acc_ref[...] += jnp.dot(vt_ref[...], a_t, preferred_element_type=F32)

    @pl.when(c == pl.num_programs(1) - 1)
    def _():
        o_ref[...] = acc_ref[...].T


def _peer(xn, q_hm, sk, u, vt, *, tt, ec):
    t, d = xn.shape
    ne = u.shape[0]
    nslab = q_hm.shape[0]
    head_arr = pltpu.VMEM((PEER_HEADS, PEER_KEYS, tt), F32)
    head_arr16 = pltpu.VMEM((PEER_HEADS, PEER_KEYS, tt), BF16)
    return pl.pallas_call(
        functools.partial(_peer_kernel, tt=tt, ec=ec),
        grid=(t // tt, ne // ec),
        in_specs=[
            pl.BlockSpec((tt, d), lambda i, c: (i, 0)),
            pl.BlockSpec((nslab, tt, LANE), lambda i, c: (0, i, 0)),
            _resident((nslab, PEER_KEYS, LANE), lambda i, c: (0, 0, 0)),
            pl.BlockSpec((ec, d), lambda i, c: (c, 0)),
            pl.BlockSpec((d, ec), lambda i, c: (0, c)),
        ],
        out_specs=pl.BlockSpec((tt, d), lambda i, c: (i, 0)),
        out_shape=jax.ShapeDtypeStruct((t, d), F32),
        scratch_shapes=[
            head_arr, head_arr16, head_arr, head_arr16,
            pltpu.VMEM((d, tt), F32),
            pltpu.VMEM((2, PEER_KEYS, tt), F32),
            pltpu.VMEM((PEER_KEYS, tt), F32),
            pltpu.VMEM((2, PEER_KEYS, tt), F32),
            pltpu.VMEM((2, PEER_TOPK, tt), F32),
            pltpu.VMEM((80, tt), F32),
            pltpu.VMEM((80, tt), F32),
        ],
        compiler_params=_cparams(("arbitrary", "arbitrary")),
        name="peer",
    )(xn, q_hm, sk, u, vt)


def _ple_kernel(h1_ref, po_ref, p_ref, g_ref, wg_ref, wp_ref, gf_ref, o_ref, *, final):
    h2 = h1_ref[...] + po_ref[...]
    xn = _rms(h2, g_ref[...]).astype(BF16)
    gate = _sigmoid(jnp.dot(xn, wg_ref[...], preferred_element_type=F32))
    h3 = h2 + gate * jnp.dot(p_ref[...].astype(BF16), wp_ref[...], preferred_element_type=F32)
    if final:
        h3 = _rms(h3, gf_ref[...])
    o_ref[...] = h3


def _ple(h1, peer_out, p, g_ple, w_gate, w_ple, g_final, *, tm, final):
    t, d = h1.shape
    pd = p.shape[1]
    return pl.pallas_call(
        functools.partial(_ple_kernel, final=final),
        grid=(t // tm,),
        in_specs=[
            pl.BlockSpec((tm, d), lambda i: (i, 0)),
            pl.BlockSpec((tm, d), lambda i: (i, 0)),
            pl.BlockSpec((tm, pd), lambda i: (i, 0)),
            pl.BlockSpec((1, d), lambda i: (0, 0)),
            _resident((d, d), lambda i: (0, 0)),
            _resident((pd, d), lambda i: (0, 0)),
            pl.BlockSpec((1, d), lambda i: (0, 0)),
        ],
        out_specs=pl.BlockSpec((tm, d), lambda i: (i, 0)),
        out_shape=jax.ShapeDtypeStruct((t, d), F32),
        compiler_params=_cparams(("arbitrary",)),
        name="ple",
    )(h1, peer_out, p, g_ple, w_gate, w_ple, g_final)


def _pick(n, pref):
    return pref if n % pref == 0 else n


def _layer(h, b, s, p_i, norm_mix_g, w_in, b_forget, w_branch_sb, w_branch_fox, w_out,
           norm_ffn_g, w_query, sub_keys, expert_u, expert_v, norm_ple_g, w_ple, w_ple_gate,
           final_norm_g, final):
    t, d = h.shape
    n_qkv = 6 * WIDTH
    w_main = jnp.concatenate([w_in[:, :n_qkv], w_in[:, n_qkv + N_HEADS:]], axis=1).astype(BF16)
    w_f = jnp.pad(w_in[:, n_qkv:n_qkv + N_HEADS], ((0, 0), (0, LANE - N_HEADS))).astype(BF16)

    proj, flog = _inproj(h, norm_mix_g.reshape(1, d), w_main, w_f, tm=_pick(t, 1024))
    proj3 = proj.reshape(b, s, proj.shape[1])

    flog_t = flog[:, :N_HEADS].reshape(b, s, N_HEADS).transpose(0, 2, 1)
    fcum = _fcum(flog_t, b_forget.reshape(N_HEADS, 1).astype(F32))
    tq = _pick(s, 512)
    o_fx = _fox(proj3, fcum.reshape(b, N_HEADS, 1, s), tq=_pick(s, 1024))
    o_sb = _sb(proj3, tq=tq)

    h1, xn = _merge(o_sb.reshape(t, WIDTH), o_fx.reshape(t, WIDTH), proj, h,
                    w_branch_sb.astype(BF16), w_branch_fox.astype(BF16), w_out.astype(BF16),
                    norm_ffn_g.reshape(1, d), tm=_pick(t, 256))

    q_hm = _qproj(xn, w_query.astype(BF16), tm=_pick(t, 512))
    sk = sub_keys.reshape(2 * PEER_HEADS, PEER_KEYS, sub_keys.shape[-1]).astype(BF16)
    peer_out = _peer(xn, q_hm, sk, expert_u.astype(BF16), expert_v.T.astype(BF16),
                     tt=_pick(t, 512), ec=1024)

    return _ple(h1, peer_out, p_i, norm_ple_g.reshape(1, d), w_ple_gate.astype(BF16),
                w_ple.astype(BF16), final_norm_g.reshape(1, d), tm=_pick(t, 256), final=final)


def kernel(x, p, norm_mix_g, w_in, b_forget, w_branch_sb, w_branch_fox, w_out, norm_ffn_g,
           w_query, sub_keys, expert_u, expert_v, norm_ple_g, w_ple, w_ple_gate, final_norm_g):
    b, s, d = x.shape
    depth = p.shape[0]
    h = x.reshape(b * s, d)
    for i in range(depth):
        h = _layer(h, b, s, p[i].reshape(b * s, -1), norm_mix_g[i], w_in[i], b_forget[i],
                   w_branch_sb[i], w_branch_fox[i], w_out[i], norm_ffn_g[i], w_query[i],
                   sub_keys[i], expert_u[i], expert_v[i], norm_ple_g[i], w_ple[i], w_ple_gate[i],
                   final_norm_g, final=(i == depth - 1))
    return h.reshape(b, s, d)
```
